```python
import math
import jax, jax.numpy as jnp
from jax import lax
import numpy as np

D_MODEL = 2048
BATCH = 8
SEQ = 2048
DEPTH = 1
DEC_BATCH = 4
DEC_SEQ = 2048
PAST_LEN = 128

HEAD_DIM = 128
N_HEADS = 8
N_KV_HEADS = 2
ATTN_WIDTH = N_HEADS * HEAD_DIM
KV_WIDTH = N_KV_HEADS * HEAD_DIM
N_SGU_GROUPS = 8
SGU_GROUP_DIM = 128
SGU_WIDTH = N_SGU_GROUPS * SGU_GROUP_DIM
MIX_WIDTH = ATTN_WIDTH + SGU_WIDTH
IN_WIDTH = ATTN_WIDTH + 2 * KV_WIDTH + 2 * SGU_WIDTH
BLOCK = 128
WINDOW = 128
D_FF = int(math.ceil(8 * D_MODEL / 3 / 256) * 256)
LN_EPS = 1e-5
NEG_INF = -1e30
DEEPNORM_ALPHA = (2.0 * DEPTH) ** 0.25
DEEPNORM_BETA = (8.0 * DEPTH) ** -0.25

kernel_name = "hybrid_sgu_window_gqa_deepnorm_encoder"


def _layer_norm(x, g, b):
    xf = x.astype(jnp.float32)
    mu = jnp.mean(xf, axis=-1, keepdims=True)
    xc = xf - mu
    var = jnp.mean(xc * xc, axis=-1, keepdims=True)
    y = xc * lax.rsqrt(var + LN_EPS) * g.astype(jnp.float32) + b.astype(jnp.float32)
    return y.astype(x.dtype)


def _alibi_slopes():
    h = np.arange(1, N_HEADS + 1, dtype=np.float32)
    return jnp.asarray(2.0 ** (-8.0 * h / N_HEADS), dtype=jnp.float32)


def _window_attention(q, k, v, sink):
    B, S, H, D = q.shape
    Hkv = k.shape[2]
    G = H // Hkv
    NC = S // BLOCK
    qb = q.reshape(B, NC, BLOCK, Hkv, G, D)
    pad = ((0, 0), (BLOCK, BLOCK), (0, 0), (0, 0))
    kp = jnp.pad(k, pad).reshape(B, NC + 2, BLOCK, Hkv, D)
    vp = jnp.pad(v, pad).reshape(B, NC + 2, BLOCK, Hkv, D)
    kb = jnp.concatenate([kp[:, :-2], kp[:, 1:-1], kp[:, 2:]], axis=2)
    vb = jnp.concatenate([vp[:, :-2], vp[:, 1:-1], vp[:, 2:]], axis=2)
    scale = 1.0 / math.sqrt(D)
    s = jnp.einsum('bcqhgd,bckhd->bchgqk', qb, kb,
                   preferred_element_type=jnp.float32) * scale
    rel = jnp.arange(3 * BLOCK)[None, :] - BLOCK - jnp.arange(BLOCK)[:, None]
    dist = jnp.abs(rel)
    kpos = jnp.arange(NC)[:, None] * BLOCK - BLOCK + jnp.arange(3 * BLOCK)[None, :]
    valid = (dist <= WINDOW)[None] & ((kpos >= 0) & (kpos < S))[:, None, :]
    slopes = _alibi_slopes().reshape(Hkv, G)
    s = s - slopes[:, :, None, None] * dist.astype(jnp.float32)
    s = jnp.where(valid[None, :, None, None], s, NEG_INF)
    sk = sink.astype(jnp.float32).reshape(Hkv, G)[None, None, :, :, None]
    m = jnp.maximum(jnp.max(s, axis=-1), sk)
    p = jnp.exp(s - m[..., None])
    den = jnp.sum(p, axis=-1) + jnp.exp(sk - m)
    w = (p / den[..., None]).astype(v.dtype)
    o = jnp.einsum('bchgqk,bckhd->bcqhgd', w, vb)
    return o.reshape(B, S, H * D)


def _chunk_sgu(z, ln_g, ln_b, w_s, b_s):
    B, S, _ = z.shape
    NC = S // BLOCK
    u, gv = z[..., :SGU_WIDTH], z[..., SGU_WIDTH:]
    gv = _layer_norm(gv, ln_g, ln_b)
    gv = gv.reshape(B, NC, BLOCK, N_SGU_GROUPS, SGU_GROUP_DIM)
    mixed = jnp.einsum('gts,bcsgd->bctgd', w_s.astype(gv.dtype), gv) + b_s.T[:, :, None].astype(gv.dtype)
    return u * mixed.reshape(B, S, SGU_WIDTH)


def _layer(x, w_in, ln_sgu_g, ln_sgu_b, w_s, b_s, attn_sink, w_o,
           ln1_g, ln1_b, w_gate, w_up, w_down, ln2_g, ln2_b):
    B, S, _ = x.shape
    h = x @ w_in
    o0 = ATTN_WIDTH
    o1 = o0 + KV_WIDTH
    o2 = o1 + KV_WIDTH
    q = h[..., :o0].reshape(B, S, N_HEADS, HEAD_DIM)
    k = h[..., o0:o1].reshape(B, S, N_KV_HEADS, HEAD_DIM)
    v = h[..., o1:o2].reshape(B, S, N_KV_HEADS, HEAD_DIM)
    attn = _window_attention(q, k, v, attn_sink)
    sgu = _chunk_sgu(jax.nn.gelu(h[..., o2:]), ln_sgu_g, ln_sgu_b, w_s, b_s)
    mix = jnp.concatenate([attn, sgu], axis=-1) @ w_o
    x = _layer_norm(DEEPNORM_ALPHA * x + mix, ln1_g, ln1_b)
    ff = (jax.nn.silu(x @ w_gate) * (x @ w_up)) @ w_down
    x = _layer_norm(DEEPNORM_ALPHA * x + ff, ln2_g, ln2_b)
    return x


def setup_inputs(seed: int = 0) -> dict:
    key = jax.random.key(seed)
    ks = jax.random.split(key, 16)
    f32 = jnp.float32
    nrm = lambda k, shape, s: jax.random.normal(k, shape, f32) * s
    return {
        "x_prompt": jax.random.normal(ks[0], (BATCH, SEQ, D_MODEL), f32),
        "x_sample": jax.random.normal(ks[1], (DEC_BATCH, DEC_SEQ, D_MODEL), f32),
        "w_in": nrm(ks[2], (DEPTH, D_MODEL, IN_WIDTH), D_MODEL ** -0.5),
        "ln_sgu_g": 1.0 + nrm(ks[3], (DEPTH, SGU_WIDTH), 0.01),
        "ln_sgu_b": nrm(ks[4], (DEPTH, SGU_WIDTH), 0.01),
        "w_s": nrm(ks[5], (DEPTH, N_SGU_GROUPS, BLOCK, BLOCK), BLOCK ** -0.5),
        "b_s": 1.0 + nrm(ks[6], (DEPTH, N_SGU_GROUPS, BLOCK), 0.01),
        "attn_sink": nrm(ks[7], (DEPTH, N_HEADS), 0.5),
        "w_o": nrm(ks[8], (DEPTH, MIX_WIDTH, D_MODEL), MIX_WIDTH ** -0.5 * DEEPNORM_BETA),
        "ln1_g": 1.0 + nrm(ks[9], (DEPTH, D_MODEL), 0.01),
        "ln1_b": nrm(ks[10], (DEPTH, D_MODEL), 0.01),
        "w_gate": nrm(ks[11], (DEPTH, D_MODEL, D_FF), D_MODEL ** -0.5),
        "w_up": nrm(ks[12], (DEPTH, D_MODEL, D_FF), D_MODEL ** -0.5),
        "w_down": nrm(ks[13], (DEPTH, D_FF, D_MODEL), D_FF ** -0.5 * DEEPNORM_BETA),
        "ln2_g": 1.0 + nrm(ks[14], (DEPTH, D_MODEL), 0.01),
        "ln2_b": nrm(ks[15], (DEPTH, D_MODEL), 0.01),
    }


def reference(x_prompt, x_sample, w_in, ln_sgu_g, ln_sgu_b, w_s, b_s, attn_sink, w_o,
              ln1_g, ln1_b, w_gate, w_up, w_down, ln2_g, ln2_b):
    y_prompt = x_prompt
    y_sample = x_sample
    for l in range(DEPTH):
        p = (w_in[l], ln_sgu_g[l], ln_sgu_b[l], w_s[l], b_s[l], attn_sink[l], w_o[l],
             ln1_g[l], ln1_b[l], w_gate[l], w_up[l], w_down[l], ln2_g[l], ln2_b[l])
        y_prompt = _layer(y_prompt, *p)
        y_sample = _layer(y_sample, *p)
    return (y_prompt, y_sample)
```

```python
import functools
import math

import numpy as np
import jax
import jax.numpy as jnp
from jax import lax
from jax.experimental import pallas as pl
from jax.experimental.pallas import tpu as pltpu

HEAD_DIM = 128
N_HEADS = 8
N_KV_HEADS = 2
GQA = N_HEADS // N_KV_HEADS
ATTN_WIDTH = N_HEADS * HEAD_DIM
KV_WIDTH = N_KV_HEADS * HEAD_DIM
QKV_WIDTH = ATTN_WIDTH + 2 * KV_WIDTH
N_SGU_GROUPS = 8
SGU_GROUP_DIM = 128
SGU_WIDTH = N_SGU_GROUPS * SGU_GROUP_DIM
BLOCK = 128
WINDOW = 128
LN_EPS = 1e-5
NEG_INF = -1e30

PROJ_TM = 512
ATTN_TQ = 512
FFN_TM = 512
FFN_TF = 512
VMEM_LIMIT = 56 * 1024 * 1024

F32 = jnp.float32
BF16 = jnp.bfloat16


def _layer_norm(x, g, b):
    mu = jnp.mean(x, axis=-1, keepdims=True)
    xc = x - mu
    var = jnp.mean(xc * xc, axis=-1, keepdims=True)
    return xc * lax.rsqrt(var + LN_EPS) * g + b


def _gelu_tanh(x):
    c = math.sqrt(2.0 / math.pi)
    return x * (0.5 * (1.0 + jnp.tanh(c * (x + 0.044715 * (x * x * x)))))


def _sigmoid(x):
    return 0.5 * (jnp.tanh(0.5 * x) + 1.0)


def _proj_kernel(x_ref, w_ref, lng_ref, lnb_ref, ws_ref, bs_ref, qkv_ref, sgu_ref):
    tm = x_ref.shape[0]
    xb = x_ref[...].astype(BF16)
    qkv = jnp.dot(xb, w_ref[:, :QKV_WIDTH], preferred_element_type=F32)
    qkv_ref[...] = qkv.astype(BF16)
    z = _gelu_tanh(jnp.dot(xb, w_ref[:, QKV_WIDTH:], preferred_element_type=F32))
    u = z[:, :SGU_WIDTH]
    gv = _layer_norm(z[:, SGU_WIDTH:], lng_ref[...], lnb_ref[...]).astype(BF16)
    for c in range(tm // BLOCK):
        rows = slice(c * BLOCK, (c + 1) * BLOCK)
        for g in range(N_SGU_GROUPS):
            cols = slice(g * SGU_GROUP_DIM, (g + 1) * SGU_GROUP_DIM)
            mixed = jnp.dot(ws_ref[g], gv[rows, cols], preferred_element_type=F32) + bs_ref[:, cols]
            sgu_ref[rows, cols] = (u[rows, cols] * mixed).astype(BF16)


def _proj_call(x2, w_in, ln_g, ln_b, w_s, bs_full):
    t, d = x2.shape
    tm = PROJ_TM
    in_w = w_in.shape[1]
    const = lambda *shape: pl.BlockSpec(shape, lambda i: (0,) * len(shape), pipeline_mode=pl.Buffered(1))
    return pl.pallas_call(
        _proj_kernel,
        grid=(t // tm,),
        in_specs=[
            pl.BlockSpec((tm, d), lambda i: (i, 0)),
            const(d, in_w),
            const(1, SGU_WIDTH),
            const(1, SGU_WIDTH),
            const(N_SGU_GROUPS, BLOCK, BLOCK),
            const(BLOCK, SGU_WIDTH),
        ],
        out_specs=[
            pl.BlockSpec((tm, QKV_WIDTH), lambda i: (i, 0)),
            pl.BlockSpec((tm, SGU_WIDTH), lambda i: (i, 0)),
        ],
        out_shape=[
            jax.ShapeDtypeStruct((t, QKV_WIDTH), BF16),
            jax.ShapeDtypeStruct((t, SGU_WIDTH), BF16),
        ],
        compiler_params=pltpu.CompilerParams(
            dimension_semantics=("arbitrary",), vmem_limit_bytes=VMEM_LIMIT),
        name="proj_sgu",
    )(x2, w_in, ln_g, ln_b, w_s, bs_full)


def _attn_kernel(sink_ref, q_ref, kp_ref, km_ref, kn_ref, vp_ref, vm_ref, vn_ref, pen_ref, o_ref):
    tq = q_ref.shape[0]
    nchunk = tq // BLOCK
    i = pl.program_id(1)
    last = pl.num_programs(1) - 1
    scale = 1.0 / math.sqrt(HEAD_DIM)
    col = lax.broadcasted_iota(jnp.int32, (1, 3 * BLOCK), 1)
    lo_pen = jnp.where((col < BLOCK) & (i == 0), NEG_INF, 0.0).astype(F32)
    hi_pen = jnp.where((col >= 2 * BLOCK) & (i == last), NEG_INF, 0.0).astype(F32)
    for h in range(N_KV_HEADS):
        hc = slice(h * HEAD_DIM, (h + 1) * HEAD_DIM)
        kk = jnp.concatenate([kp_ref[:, hc], km_ref[:, hc], kn_ref[:, hc]], axis=0)
        vv = jnp.concatenate([vp_ref[:, hc], vm_ref[:, hc], vn_ref[:, hc]], axis=0)
        pen = pen_ref[h * GQA * BLOCK:(h + 1) * GQA * BLOCK, :]
        for c in range(nchunk):
            rows = slice(c * BLOCK, (c + 1) * BLOCK)
            qs = jnp.concatenate(
                [q_ref[rows, (h * GQA + g) * HEAD_DIM:(h * GQA + g + 1) * HEAD_DIM] for g in range(GQA)],
                axis=0)
            kc = kk[c * BLOCK:(c + 3) * BLOCK]
            vc = vv[c * BLOCK:(c + 3) * BLOCK]
            s = lax.dot_general(qs, kc, (((1,), (1,)), ((), ())), preferred_element_type=F32)
            s = s * scale + pen
            if c == 0:
                s = s + lo_pen
            if c == nchunk - 1:
                s = s + hi_pen
            ws = []
            for g in range(GQA):
                sg = s[g * BLOCK:(g + 1) * BLOCK]
                sk = sink_ref[h * GQA + g]
                m = jnp.maximum(jnp.max(sg, axis=-1, keepdims=True), sk)
                p = jnp.exp(sg - m)
                den = jnp.sum(p, axis=-1, keepdims=True) + jnp.exp(sk - m)
                ws.append((p * (1.0 / den)).astype(BF16))
            w = jnp.concatenate(ws, axis=0)
            o = jnp.dot(w, vc, preferred_element_type=F32)
            for g in range(GQA):
                o_ref[rows, (h * GQA + g) * HEAD_DIM:(h * GQA + g + 1) * HEAD_DIM] = (
                    o[g * BLOCK:(g + 1) * BLOCK].astype(BF16))


def _alibi_penalty():
    slopes = 2.0 ** (-8.0 * np.arange(1, N_HEADS + 1, dtype=np.float32) / N_HEADS)
    rel = np.arange(3 * BLOCK)[None, :] - BLOCK - np.arange(BLOCK)[:, None]
    dist = np.abs(rel).astype(np.float32)
    pen = np.where(dist[None] <= WINDOW, -slopes[:, None, None] * dist[None], np.float32(NEG_INF))
    return pen.reshape(N_HEADS * BLOCK, 3 * BLOCK).astype(np.float32)


def _attn_call(qkv, sink):
    b, s, _ = qkv.shape
    tq = ATTN_TQ
    r = tq // BLOCK
    nc = s // BLOCK
    kcol = ATTN_WIDTH // KV_WIDTH
    vcol = kcol + 1
    prev = lambda bi, i: jnp.maximum(i * r - 1, 0)
    nxt = lambda bi, i: jnp.minimum((i + 1) * r, nc - 1)
    pen = jnp.asarray(_alibi_penalty())
    return pl.pallas_call(
        _attn_kernel,
        grid=(b, s // tq),
        in_specs=[
            pl.BlockSpec(memory_space=pltpu.SMEM),
            pl.BlockSpec((None, tq, ATTN_WIDTH), lambda bi, i: (bi, i, 0)),
            pl.BlockSpec((None, BLOCK, KV_WIDTH), lambda bi, i: (bi, prev(bi, i), kcol)),
            pl.BlockSpec((None, tq, KV_WIDTH), lambda bi, i: (bi, i, kcol)),
            pl.BlockSpec((None, BLOCK, KV_WIDTH), lambda bi, i: (bi, nxt(bi, i), kcol)),
            pl.BlockSpec((None, BLOCK, KV_WIDTH), lambda bi, i: (bi, prev(bi, i), vcol)),
            pl.BlockSpec((None, tq, KV_WIDTH), lambda bi, i: (bi, i, vcol)),
            pl.BlockSpec((None, BLOCK, KV_WIDTH), lambda bi, i: (bi, nxt(bi, i), vcol)),
            pl.BlockSpec(pen.shape, lambda bi, i: (0, 0), pipeline_mode=pl.Buffered(1)),
        ],
        out_specs=pl.BlockSpec((None, tq, ATTN_WIDTH), lambda bi, i: (bi, i, 0)),
        out_shape=jax.ShapeDtypeStruct((b, s, ATTN_WIDTH), BF16),
        compiler_params=pltpu.CompilerParams(
            dimension_semantics=("arbitrary", "arbitrary"), vmem_limit_bytes=VMEM_LIMIT),
        name="window_attn",
    )(sink, qkv, qkv, qkv, qkv, qkv, qkv, qkv, pen)


def _ffn_kernel(alpha, x_ref, attn_ref, sgu_ref, wo_ref, g1_ref, b1_ref, wg_ref, wu_ref, wd_ref,
                g2_ref, b2_ref, out_ref, x1b_ref):
    j = pl.program_id(1)

    @pl.when(j == 0)
    def _():
        mix = jnp.dot(attn_ref[...], wo_ref[:ATTN_WIDTH, :], preferred_element_type=F32)
        mix = mix + jnp.dot(sgu_ref[...], wo_ref[ATTN_WIDTH:, :], preferred_element_type=F32)
        x1 = _layer_norm(alpha * x_ref[...] + mix, g1_ref[...], b1_ref[...])
        x1b_ref[...] = x1.astype(BF16)
        out_ref[...] = alpha * x1

    xb = x1b_ref[...]
    gate = jnp.dot(xb, wg_ref[...], preferred_element_type=F32)
    up = jnp.dot(xb, wu_ref[...], preferred_element_type=F32)
    act = (gate * _sigmoid(gate) * up).astype(BF16)
    out_ref[...] += jnp.dot(act, wd_ref[...], preferred_element_type=F32)

    @pl.when(j == pl.num_programs(1) - 1)
    def _():
        out_ref[...] = _layer_norm(out_ref[...], g2_ref[...], b2_ref[...])


def _ffn_call(alpha, x2, attn, sgu, w_o, g1, b1, w_gate, w_up, w_down, g2, b2):
    t, d = x2.shape
    d_ff = w_gate.shape[1]
    tm, tf = FFN_TM, FFN_TF
    const = lambda *shape: pl.BlockSpec(shape, lambda i, j: (0,) * len(shape), pipeline_mode=pl.Buffered(1))
    return pl.pallas_call(
        functools.partial(_ffn_kernel, alpha),
        grid=(t // tm, d_ff // tf),
        in_specs=[
            pl.BlockSpec((tm, d), lambda i, j: (i, 0)),
            pl.BlockSpec((tm, ATTN_WIDTH), lambda i, j: (i, 0)),
            pl.BlockSpec((tm, SGU_WIDTH), lambda i, j: (i, 0)),
            const(ATTN_WIDTH + SGU_WIDTH, d),
            const(1, d),
            const(1, d),
            pl.BlockSpec((d, tf), lambda i, j: (0, j)),
            pl.BlockSpec((d, tf), lambda i, j: (0, j)),
            pl.BlockSpec((tf, d), lambda i, j: (j, 0)),
            const(1, d),
            const(1, d),
        ],
        out_specs=pl.BlockSpec((tm, d), lambda i, j: (i, 0)),
        out_shape=jax.ShapeDtypeStruct((t, d), F32),
        scratch_shapes=[pltpu.VMEM((tm, d), BF16)],
        compiler_params=pltpu.CompilerParams(
            dimension_semantics=("arbitrary", "arbitrary"), vmem_limit_bytes=VMEM_LIMIT),
        name="wo_ffn",
    )(x2, attn, sgu, w_o, g1, b1, w_gate, w_up, w_down, g2, b2)


def _layer(alpha, x, w_in, ln_sgu_g, ln_sgu_b, w_s, bs_full, sink, w_o, g1, b1, w_gate, w_up, w_down, g2, b2):
    b, s, d = x.shape
    x2 = x.reshape(b * s, d)
    qkv, sgu = _proj_call(x2, w_in, ln_sgu_g, ln_sgu_b, w_s, bs_full)
    attn = _attn_call(qkv.reshape(b, s, QKV_WIDTH), sink)
    out = _ffn_call(alpha, x2, attn.reshape(b * s, ATTN_WIDTH), sgu, w_o, g1, b1, w_gate, w_up, w_down, g2, b2)
    return out.reshape(b, s, d)


def kernel(x_prompt, x_sample, w_in, ln_sgu_g, ln_sgu_b, w_s, b_s, attn_sink, w_o, ln1_g, ln1_b,
           w_gate, w_up, w_down, ln2_g, ln2_b):
    depth = w_in.shape[0]
    alpha = (2.0 * depth) ** 0.25
    row = lambda v: v.reshape(1, -1).astype(F32)
    y_prompt, y_sample = x_prompt, x_sample
    for l in range(depth):
        bs_full = jnp.repeat(b_s[l].T.astype(F32), SGU_GROUP_DIM, axis=1)
        params = (
            w_in[l].astype(BF16), row(ln_sgu_g[l]), row(ln_sgu_b[l]), w_s[l].astype(BF16), bs_full,
            attn_sink[l].astype(F32), w_o[l].astype(BF16), row(ln1_g[l]), row(ln1_b[l]),
            w_gate[l].astype(BF16), w_up[l].astype(BF16), w_down[l].astype(BF16),
            row(ln2_g[l]), row(ln2_b[l]),
        )
        y_prompt = _layer(alpha, y_prompt, *params)
        y_sample = _layer(alpha, y_sample, *params)
    return (y_prompt, y_sample)
```

```python
import functools
import math

import numpy as np
import jax
import jax.numpy as jnp
from jax import lax
from jax.experimental import pallas as pl
from jax.experimental.pallas import tpu as pltpu

HEAD_DIM = 128
N_HEADS = 8
N_KV_HEADS = 2
GQA = N_HEADS // N_KV_HEADS
ATTN_WIDTH = N_HEADS * HEAD_DIM
KV_WIDTH = N_KV_HEADS * HEAD_DIM
QKV_WIDTH = ATTN_WIDTH + 2 * KV_WIDTH
N_SGU_GROUPS = 8
SGU_GROUP_DIM = 128
SGU_WIDTH = N_SGU_GROUPS * SGU_GROUP_DIM
BLOCK = 128
WINDOW = 128
LN_EPS = 1e-5
NEG_INF = -1e30

PROJ_TM = 512
ATTN_TQ = 512
ATTN_WO_ROWS = 256
ATTN_WO_COLS = 512
FFN_TM = 1024
FFN_TF = 512
VMEM_LIMIT = 56 * 1024 * 1024
FFN_VMEM_LIMIT = 62 * 1024 * 1024

F32 = jnp.float32
BF16 = jnp.bfloat16


def _layer_norm(x, g, b):
    mu = jnp.mean(x, axis=-1, keepdims=True)
    xc = x - mu
    var = jnp.mean(xc * xc, axis=-1, keepdims=True)
    return xc * lax.rsqrt(var + LN_EPS) * g + b


def _gelu_tanh(x):
    c = math.sqrt(2.0 / math.pi)
    return x * (0.5 * (1.0 + jnp.tanh(c * (x + 0.044715 * (x * x * x)))))


def _sigmoid(x):
    return 0.5 * (jnp.tanh(0.5 * x) + 1.0)


def _proj_kernel(x_ref, w_ref, lng_ref, lnb_ref, ws_ref, bs_ref, qkv_ref, sgu_ref):
    tm = x_ref.shape[0]
    xb = x_ref[...].astype(BF16)
    qkv = jnp.dot(xb, w_ref[:, :QKV_WIDTH], preferred_element_type=F32)
    qkv_ref[...] = qkv.astype(BF16)
    z = _gelu_tanh(jnp.dot(xb, w_ref[:, QKV_WIDTH:], preferred_element_type=F32))
    u = z[:, :SGU_WIDTH]
    gv = _layer_norm(z[:, SGU_WIDTH:], lng_ref[...], lnb_ref[...]).astype(BF16)
    for c in range(tm // BLOCK):
        rows = slice(c * BLOCK, (c + 1) * BLOCK)
        for g in range(N_SGU_GROUPS):
            cols = slice(g * SGU_GROUP_DIM, (g + 1) * SGU_GROUP_DIM)
            mixed = jnp.dot(ws_ref[g], gv[rows, cols], preferred_element_type=F32) + bs_ref[:, cols]
            sgu_ref[rows, cols] = (u[rows, cols] * mixed).astype(BF16)


def _proj_call(x2, w_in, ln_g, ln_b, w_s, bs_full):
    t, d = x2.shape
    tm = PROJ_TM
    in_w = w_in.shape[1]
    const = lambda *shape: pl.BlockSpec(shape, lambda i: (0,) * len(shape), pipeline_mode=pl.Buffered(1))
    return pl.pallas_call(
        _proj_kernel,
        grid=(t // tm,),
        in_specs=[
            pl.BlockSpec((tm, d), lambda i: (i, 0)),
            const(d, in_w),
            const(1, SGU_WIDTH),
            const(1, SGU_WIDTH),
            const(N_SGU_GROUPS, BLOCK, BLOCK),
            const(BLOCK, SGU_WIDTH),
        ],
        out_specs=[
            pl.BlockSpec((tm, QKV_WIDTH), lambda i: (i, 0)),
            pl.BlockSpec((tm, SGU_WIDTH), lambda i: (i, 0)),
        ],
        out_shape=[
            jax.ShapeDtypeStruct((t, QKV_WIDTH), BF16),
            jax.ShapeDtypeStruct((t, SGU_WIDTH), BF16),
        ],
        compiler_params=pltpu.CompilerParams(
            dimension_semantics=("arbitrary",), vmem_limit_bytes=VMEM_LIMIT),
        name="proj_sgu",
    )(x2, w_in, ln_g, ln_b, w_s, bs_full)


def _attn_unit(q_ref, kk, vv, pen_ref, sink_ref, c, h, lo_pen, hi_pen):
    scale = 1.0 / math.sqrt(HEAD_DIM)
    rows = slice(c * BLOCK, (c + 1) * BLOCK)
    qs = jnp.concatenate(
        [q_ref[rows, (h * GQA + g) * HEAD_DIM:(h * GQA + g + 1) * HEAD_DIM] for g in range(GQA)], axis=0)
    kc = kk[h][c * BLOCK:(c + 3) * BLOCK]
    vc = vv[h][c * BLOCK:(c + 3) * BLOCK]
    s = lax.dot_general(qs, kc, (((1,), (1,)), ((), ())), preferred_element_type=F32)
    s = s * scale + pen_ref[h * GQA * BLOCK:(h + 1) * GQA * BLOCK, :]
    if lo_pen is not None:
        s = s + lo_pen
    if hi_pen is not None:
        s = s + hi_pen
    ps, rdens = [], []
    for g in range(GQA):
        sg = s[g * BLOCK:(g + 1) * BLOCK]
        sk = sink_ref[h * GQA + g]
        m = jnp.maximum(jnp.max(sg, axis=-1, keepdims=True), sk)
        p = jnp.exp(sg - m)
        rdens.append(1.0 / (jnp.sum(p, axis=-1, keepdims=True) + jnp.exp(sk - m)))
        ps.append(p.astype(BF16))
    o = jnp.dot(jnp.concatenate(ps, axis=0), vc, preferred_element_type=F32)
    return [(o[g * BLOCK:(g + 1) * BLOCK] * rdens[g]).astype(BF16) for g in range(GQA)]


def _attn_wo_kernel(alpha, sink_ref, q_ref, kp_ref, km_ref, kn_ref, vp_ref, vm_ref, vn_ref, pen_ref,
                    sgu_ref, x_ref, wo_ref, g1_ref, b1_ref, x1_ref):
    tq = q_ref.shape[0]
    d = x_ref.shape[1]
    nchunk = tq // BLOCK
    sub = ATTN_WO_ROWS // BLOCK
    nblk = nchunk // sub
    i = pl.program_id(1)
    last = pl.num_programs(1) - 1
    col = lax.broadcasted_iota(jnp.int32, (1, 3 * BLOCK), 1)
    lo_pen = jnp.where((col < BLOCK) & (i == 0), NEG_INF, 0.0).astype(F32)
    hi_pen = jnp.where((col >= 2 * BLOCK) & (i == last), NEG_INF, 0.0).astype(F32)
    kk, vv = [], []
    for h in range(N_KV_HEADS):
        hc = slice(h * HEAD_DIM, (h + 1) * HEAD_DIM)
        kk.append(jnp.concatenate([kp_ref[:, hc], km_ref[:, hc], kn_ref[:, hc]], axis=0))
        vv.append(jnp.concatenate([vp_ref[:, hc], vm_ref[:, hc], vn_ref[:, hc]], axis=0))

    heads = {}

    def attn_unit(c, h):
        outs = _attn_unit(q_ref, kk, vv, pen_ref, sink_ref, c, h,
                          lo_pen if c == 0 else None, hi_pen if c == nchunk - 1 else None)
        for g in range(GQA):
            heads[(c, h * GQA + g)] = outs[g]

    def units_of(blk):
        return [(c, h) for c in range(blk * sub, (blk + 1) * sub) for h in range(N_KV_HEADS)]

    def wo_piece(blk, attn, n):
        rows = slice(blk * ATTN_WO_ROWS, (blk + 1) * ATTN_WO_ROWS)
        cols = slice(n * ATTN_WO_COLS, (n + 1) * ATTN_WO_COLS)
        piece = jnp.dot(attn, wo_ref[:ATTN_WIDTH, cols], preferred_element_type=F32)
        return piece + jnp.dot(sgu_ref[rows, :], wo_ref[ATTN_WIDTH:, cols], preferred_element_type=F32)

    def norm_store(blk, pieces):
        rows = slice(blk * ATTN_WO_ROWS, (blk + 1) * ATTN_WO_ROWS)
        mix = jnp.concatenate(pieces, axis=1)
        x1_ref[rows, :] = _layer_norm(alpha * x_ref[rows, :] + mix, g1_ref[...], b1_ref[...])

    for c, h in units_of(0):
        attn_unit(c, h)
    prev_pieces = None
    for blk in range(nblk):
        attn = jnp.concatenate(
            [jnp.concatenate([heads[(c, hd)] for hd in range(N_HEADS)], axis=1)
             for c in range(blk * sub, (blk + 1) * sub)], axis=0)
        pending = units_of(blk + 1) if blk + 1 < nblk else []
        npiece = d // ATTN_WO_COLS
        pieces = []
        norm_at = len(pending) if len(pending) < npiece else 0
        for n in range(npiece):
            pieces.append(wo_piece(blk, attn, n))
            if n < len(pending):
                attn_unit(*pending[n])
            if n == norm_at and prev_pieces is not None:
                norm_store(blk - 1, prev_pieces)
        for c, h in pending[npiece:]:
            attn_unit(c, h)
        prev_pieces = pieces
    norm_store(nblk - 1, prev_pieces)


def _alibi_penalty():
    slopes = 2.0 ** (-8.0 * np.arange(1, N_HEADS + 1, dtype=np.float32) / N_HEADS)
    rel = np.arange(3 * BLOCK)[None, :] - BLOCK - np.arange(BLOCK)[:, None]
    dist = np.abs(rel).astype(np.float32)
    pen = np.where(dist[None] <= WINDOW, -slopes[:, None, None] * dist[None], np.float32(NEG_INF))
    return pen.reshape(N_HEADS * BLOCK, 3 * BLOCK).astype(np.float32)


def _attn_wo_call(alpha, qkv, sink, sgu, x, w_o, g1, b1):
    b, s, d = x.shape
    tq = ATTN_TQ
    r = tq // BLOCK
    nc = s // BLOCK
    kcol = ATTN_WIDTH // KV_WIDTH
    vcol = kcol + 1
    prev = lambda i: jnp.maximum(i * r - 1, 0)
    nxt = lambda i: jnp.minimum((i + 1) * r, nc - 1)
    pen = jnp.asarray(_alibi_penalty())
    const = lambda *shape: pl.BlockSpec(shape, lambda bi, i: (0,) * len(shape), pipeline_mode=pl.Buffered(1))
    tile = lambda width: pl.BlockSpec((None, tq, width), lambda bi, i: (bi, i, 0))
    return pl.pallas_call(
        functools.partial(_attn_wo_kernel, alpha),
        grid=(b, s // tq),
        in_specs=[
            pl.BlockSpec(memory_space=pltpu.SMEM),
            tile(ATTN_WIDTH),
            pl.BlockSpec((None, BLOCK, KV_WIDTH), lambda bi, i: (bi, prev(i), kcol)),
            pl.BlockSpec((None, tq, KV_WIDTH), lambda bi, i: (bi, i, kcol)),
            pl.BlockSpec((None, BLOCK, KV_WIDTH), lambda bi, i: (bi, nxt(i), kcol)),
            pl.BlockSpec((None, BLOCK, KV_WIDTH), lambda bi, i: (bi, prev(i), vcol)),
            pl.BlockSpec((None, tq, KV_WIDTH), lambda bi, i: (bi, i, vcol)),
            pl.BlockSpec((None, BLOCK, KV_WIDTH), lambda bi, i: (bi, nxt(i), vcol)),
            const(*pen.shape),
            tile(SGU_WIDTH),
            tile(d),
            const(ATTN_WIDTH + SGU_WIDTH, d),
            const(1, d),
            const(1, d),
        ],
        out_specs=tile(d),
        out_shape=jax.ShapeDtypeStruct((b, s, d), F32),
        compiler_params=pltpu.CompilerParams(
            dimension_semantics=("arbitrary", "arbitrary"), vmem_limit_bytes=VMEM_LIMIT),
        name="attn_wo",
    )(sink, qkv, qkv, qkv, qkv, qkv, qkv, qkv, pen, sgu, x, w_o, g1, b1)


def _ffn_kernel(alpha, x1_ref, wg_ref, wu_ref, wd_ref, g2_ref, b2_ref, out_ref, x1b_ref):
    j = pl.program_id(1)

    @pl.when(j == 0)
    def _():
        x1 = x1_ref[...]
        x1b_ref[...] = x1.astype(BF16)
        out_ref[...] = alpha * x1

    xb = x1b_ref[...]
    gate = jnp.dot(xb, wg_ref[...], preferred_element_type=F32)
    up = jnp.dot(xb, wu_ref[...], preferred_element_type=F32)
    act = (gate * _sigmoid(gate) * up).astype(BF16)
    out_ref[...] += jnp.dot(act, wd_ref[...], preferred_element_type=F32)

    @pl.when(j == pl.num_programs(1) - 1)
    def _():
        out_ref[...] = _layer_norm(out_ref[...], g2_ref[...], b2_ref[...])


def _ffn_call(alpha, x1, w_gate, w_up, w_down, g2, b2):
    t, d = x1.shape
    d_ff = w_gate.shape[1]
    tm, tf = FFN_TM, FFN_TF
    const = lambda *shape: pl.BlockSpec(shape, lambda i, j: (0,) * len(shape), pipeline_mode=pl.Buffered(1))
    return pl.pallas_call(
        functools.partial(_ffn_kernel, alpha),
        grid=(t // tm, d_ff // tf),
        in_specs=[
            pl.BlockSpec((tm, d), lambda i, j: (i, 0)),
            pl.BlockSpec((d, tf), lambda i, j: (0, j)),
            pl.BlockSpec((d, tf), lambda i, j: (0, j)),
            pl.BlockSpec((tf, d), lambda i, j: (j, 0)),
            const(1, d),
            const(1, d),
        ],
        out_specs=pl.BlockSpec((tm, d), lambda i, j: (i, 0)),
        out_shape=jax.ShapeDtypeStruct((t, d), F32),
        scratch_shapes=[pltpu.VMEM((tm, d), BF16)],
        compiler_params=pltpu.CompilerParams(
            dimension_semantics=("arbitrary", "arbitrary"), vmem_limit_bytes=FFN_VMEM_LIMIT),
        name="ffn",
    )(x1, w_gate, w_up, w_down, g2, b2)


def _layer(alpha, x, w_in, ln_sgu_g, ln_sgu_b, w_s, bs_full, sink, w_o, g1, b1, w_gate, w_up, w_down, g2, b2):
    b, s, d = x.shape
    qkv, sgu = _proj_call(x.reshape(b * s, d), w_in, ln_sgu_g, ln_sgu_b, w_s, bs_full)
    x1 = _attn_wo_call(alpha, qkv.reshape(b, s, QKV_WIDTH), sink, sgu.reshape(b, s, SGU_WIDTH), x, w_o, g1, b1)
    out = _ffn_call(alpha, x1.reshape(b * s, d), w_gate, w_up, w_down, g2, b2)
    return out.reshape(b, s, d)


def kernel(x_prompt, x_sample, w_in, ln_sgu_g, ln_sgu_b, w_s, b_s, attn_sink, w_o, ln1_g, ln1_b,
           w_gate, w_up, w_down, ln2_g, ln2_b):
    depth = w_in.shape[0]
    alpha = (2.0 * depth) ** 0.25
    row = lambda v: v.reshape(1, -1).astype(F32)
    y_prompt, y_sample = x_prompt, x_sample
    for l in range(depth):
        bs_full = jnp.repeat(b_s[l].T.astype(F32), SGU_GROUP_DIM, axis=1)
        params = (
            w_in[l].astype(BF16), row(ln_sgu_g[l]), row(ln_sgu_b[l]), w_s[l].astype(BF16), bs_full,
            attn_sink[l].astype(F32), w_o[l].astype(BF16), row(ln1_g[l]), row(ln1_b[l]),
            w_gate[l].astype(BF16), w_up[l].astype(BF16), w_down[l].astype(BF16),
            row(ln2_g[l]), row(ln2_b[l]),
        )
        y_prompt = _layer(alpha, y_prompt, *params)
        y_sample = _layer(alpha, y_sample, *params)
    return (y_prompt, y_sample)
```

```python
import functools
import math

import numpy as np
import jax
import jax.numpy as jnp
from jax import lax
from jax.experimental import pallas as pl
from jax.experimental.pallas import tpu as pltpu

HEAD_DIM = 128
N_HEADS = 8
N_KV_HEADS = 2
GQA = N_HEADS // N_KV_HEADS
ATTN_WIDTH = N_HEADS * HEAD_DIM
KV_WIDTH = N_KV_HEADS * HEAD_DIM
QKV_WIDTH = ATTN_WIDTH + 2 * KV_WIDTH
N_SGU_GROUPS = 8
SGU_GROUP_DIM = 128
SGU_WIDTH = N_SGU_GROUPS * SGU_GROUP_DIM
BLOCK = 128
WINDOW = 128
LN_EPS = 1e-5
NEG_INF = -1e30
LOG2E = math.log2(math.e)
BF16_SUBLANES = 16

PROJ_TM = 512
ATTN_TQ = 512
ATTN_WO_ROWS = 256
ATTN_WO_COLS = 512
FFN_TM = 1024
FFN_TF = 512
VMEM_LIMIT = 56 * 1024 * 1024
FFN_VMEM_LIMIT = 62 * 1024 * 1024

F32 = jnp.float32
BF16 = jnp.bfloat16


def _layer_norm(x, g, b):
    mu = jnp.mean(x, axis=-1, keepdims=True)
    xc = x - mu
    var = jnp.mean(xc * xc, axis=-1, keepdims=True)
    return xc * lax.rsqrt(var + LN_EPS) * g + b


def _gelu_tanh(x):
    c = math.sqrt(2.0 / math.pi)
    return x * (0.5 * (1.0 + jnp.tanh(c * (x + 0.044715 * (x * x * x)))))


def _sigmoid(x):
    return 0.5 * (jnp.tanh(0.5 * x) + 1.0)


def _proj_kernel(n_cast, x_ref, w_ref, lng_ref, lnb_ref, ws_ref, bs_ref, *refs):
    cast_in, (qkv_ref, sgu_ref), cast_out = refs[:n_cast], refs[n_cast:n_cast + 2], refs[n_cast + 2:]
    tm = x_ref.shape[0]
    xb = x_ref[...].astype(BF16)
    qkv = jnp.dot(xb, w_ref[:, :QKV_WIDTH], preferred_element_type=F32)
    qkv_ref[:, :ATTN_WIDTH] = (qkv[:, :ATTN_WIDTH] * (LOG2E / math.sqrt(HEAD_DIM))).astype(BF16)
    qkv_ref[:, ATTN_WIDTH:] = qkv[:, ATTN_WIDTH:].astype(BF16)
    z = _gelu_tanh(jnp.dot(xb, w_ref[:, QKV_WIDTH:], preferred_element_type=F32))
    u = z[:, :SGU_WIDTH]
    gv = _layer_norm(z[:, SGU_WIDTH:], lng_ref[...], lnb_ref[...]).astype(BF16)
    for c in range(tm // BLOCK):
        rows = slice(c * BLOCK, (c + 1) * BLOCK)
        for g in range(N_SGU_GROUPS):
            cols = slice(g * SGU_GROUP_DIM, (g + 1) * SGU_GROUP_DIM)
            mixed = jnp.dot(ws_ref[g], gv[rows, cols], preferred_element_type=F32) + bs_ref[:, cols]
            sgu_ref[rows, cols] = (u[rows, cols] * mixed).astype(BF16)
    for src, dst in zip(cast_in, cast_out):
        dst[...] = src[...].astype(BF16)


def _proj_call(x2, w_in, ln_g, ln_b, w_s, bs_full, cast=()):
    t, d = x2.shape
    tm = PROJ_TM
    steps = t // tm
    in_w = w_in.shape[1]
    const = lambda *shape: pl.BlockSpec(shape, lambda i: (0,) * len(shape), pipeline_mode=pl.Buffered(1))
    for w in cast:
        assert w.shape[0] % (steps * BF16_SUBLANES) == 0, (w.shape, steps)
    slab = lambda w: pl.BlockSpec((w.shape[0] // steps, w.shape[1]), lambda i: (i, 0))
    return pl.pallas_call(
        functools.partial(_proj_kernel, len(cast)),
        grid=(steps,),
        in_specs=[
            pl.BlockSpec((tm, d), lambda i: (i, 0)),
            const(d, in_w),
            const(1, SGU_WIDTH),
            const(1, SGU_WIDTH),
            const(N_SGU_GROUPS, BLOCK, BLOCK),
            const(BLOCK, SGU_WIDTH),
        ] + [slab(w) for w in cast],
        out_specs=[
            pl.BlockSpec((tm, QKV_WIDTH), lambda i: (i, 0)),
            pl.BlockSpec((tm, SGU_WIDTH), lambda i: (i, 0)),
        ] + [slab(w) for w in cast],
        out_shape=[
            jax.ShapeDtypeStruct((t, QKV_WIDTH), BF16),
            jax.ShapeDtypeStruct((t, SGU_WIDTH), BF16),
        ] + [jax.ShapeDtypeStruct(w.shape, BF16) for w in cast],
        compiler_params=pltpu.CompilerParams(
            dimension_semantics=("arbitrary",), vmem_limit_bytes=VMEM_LIMIT),
        name="proj_sgu",
    )(x2, w_in, ln_g, ln_b, w_s, bs_full, *cast)


def _attn_unit(q_ref, kk, vv, pen_ref, sink_ref, c, h, lo_pen, hi_pen):
    rows = slice(c * BLOCK, (c + 1) * BLOCK)
    qs = jnp.concatenate(
        [q_ref[rows, (h * GQA + g) * HEAD_DIM:(h * GQA + g + 1) * HEAD_DIM] for g in range(GQA)], axis=0)
    kc = kk[h][c * BLOCK:(c + 3) * BLOCK]
    vc = vv[h][c * BLOCK:(c + 3) * BLOCK]
    s = lax.dot_general(qs, kc, (((1,), (1,)), ((), ())), preferred_element_type=F32)
    s = s + pen_ref[h * GQA * BLOCK:(h + 1) * GQA * BLOCK, :]
    if lo_pen is not None:
        s = s + lo_pen
    if hi_pen is not None:
        s = s + hi_pen
    ps, rdens = [], []
    for g in range(GQA):
        sg = s[g * BLOCK:(g + 1) * BLOCK]
        sk = sink_ref[h * GQA + g] * LOG2E
        m = jnp.maximum(jnp.max(sg, axis=-1, keepdims=True), sk)
        p = jnp.exp2(sg - m)
        rdens.append(1.0 / (jnp.sum(p, axis=-1, keepdims=True) + jnp.exp2(sk - m)))
        ps.append(p.astype(BF16))
    o = jnp.dot(jnp.concatenate(ps, axis=0), vc, preferred_element_type=F32)
    return [(o[g * BLOCK:(g + 1) * BLOCK] * rdens[g]).astype(BF16) for g in range(GQA)]


def _attn_wo_kernel(alpha, sink_ref, q_ref, kp_ref, km_ref, kn_ref, vp_ref, vm_ref, vn_ref, pen_ref,
                    sgu_ref, x_ref, wo_ref, g1_ref, b1_ref, x1_ref):
    tq = q_ref.shape[0]
    d = x_ref.shape[1]
    nchunk = tq // BLOCK
    sub = ATTN_WO_ROWS // BLOCK
    nblk = nchunk // sub
    i = pl.program_id(1)
    last = pl.num_programs(1) - 1
    col = lax.broadcasted_iota(jnp.int32, (1, 3 * BLOCK), 1)
    lo_pen = jnp.where((col < BLOCK) & (i == 0), NEG_INF, 0.0).astype(F32)
    hi_pen = jnp.where((col >= 2 * BLOCK) & (i == last), NEG_INF, 0.0).astype(F32)
    kk, vv = [], []
    for h in range(N_KV_HEADS):
        hc = slice(h * HEAD_DIM, (h + 1) * HEAD_DIM)
        kk.append(jnp.concatenate([kp_ref[:, hc], km_ref[:, hc], kn_ref[:, hc]], axis=0))
        vv.append(jnp.concatenate([vp_ref[:, hc], vm_ref[:, hc], vn_ref[:, hc]], axis=0))

    heads = {}

    def attn_unit(c, h):
        outs = _attn_unit(q_ref, kk, vv, pen_ref, sink_ref, c, h,
                          lo_pen if c == 0 else None, hi_pen if c == nchunk - 1 else None)
        for g in range(GQA):
            heads[(c, h * GQA + g)] = outs[g]

    def units_of(blk):
        return [(c, h) for c in range(blk * sub, (blk + 1) * sub) for h in range(N_KV_HEADS)]

    def wo_piece(blk, attn, n):
        rows = slice(blk * ATTN_WO_ROWS, (blk + 1) * ATTN_WO_ROWS)
        cols = slice(n * ATTN_WO_COLS, (n + 1) * ATTN_WO_COLS)
        piece = jnp.dot(attn, wo_ref[:ATTN_WIDTH, cols], preferred_element_type=F32)
        return piece + jnp.dot(sgu_ref[rows, :], wo_ref[ATTN_WIDTH:, cols], preferred_element_type=F32)

    def norm_store(blk, pieces):
        rows = slice(blk * ATTN_WO_ROWS, (blk + 1) * ATTN_WO_ROWS)
        mix = jnp.concatenate(pieces, axis=1)
        x1_ref[rows, :] = _layer_norm(alpha * x_ref[rows, :] + mix, g1_ref[...], b1_ref[...])

    for c, h in units_of(0):
        attn_unit(c, h)
    prev_pieces = None
    for blk in range(nblk):
        attn = jnp.concatenate(
            [jnp.concatenate([heads[(c, hd)] for hd in range(N_HEADS)], axis=1)
             for c in range(blk * sub, (blk + 1) * sub)], axis=0)
        pending = units_of(blk + 1) if blk + 1 < nblk else []
        npiece = d // ATTN_WO_COLS
        pieces = []
        norm_at = len(pending) if len(pending) < npiece else 0
        for n in range(npiece):
            pieces.append(wo_piece(blk, attn, n))
            if n < len(pending):
                attn_unit(*pending[n])
            if n == norm_at and prev_pieces is not None:
                norm_store(blk - 1, prev_pieces)
        for c, h in pending[npiece:]:
            attn_unit(c, h)
        prev_pieces = pieces
    norm_store(nblk - 1, prev_pieces)


def _alibi_penalty():
    slopes = 2.0 ** (-8.0 * np.arange(1, N_HEADS + 1, dtype=np.float64) / N_HEADS)
    rel = np.arange(3 * BLOCK)[None, :] - BLOCK - np.arange(BLOCK)[:, None]
    dist = np.abs(rel).astype(np.float64)
    pen = np.where(dist[None] <= WINDOW, -slopes[:, None, None] * dist[None] * LOG2E, NEG_INF)
    return pen.reshape(N_HEADS * BLOCK, 3 * BLOCK).astype(np.float32)


def _attn_wo_call(alpha, qkv, sink, sgu, x, w_o, g1, b1):
    b, s, d = x.shape
    tq = ATTN_TQ
    r = tq // BLOCK
    nc = s // BLOCK
    kcol = ATTN_WIDTH // KV_WIDTH
    vcol = kcol + 1
    prev = lambda i: jnp.maximum(i * r - 1, 0)
    nxt = lambda i: jnp.minimum((i + 1) * r, nc - 1)
    pen = jnp.asarray(_alibi_penalty())
    const = lambda *shape: pl.BlockSpec(shape, lambda bi, i: (0,) * len(shape), pipeline_mode=pl.Buffered(1))
    tile = lambda width: pl.BlockSpec((None, tq, width), lambda bi, i: (bi, i, 0))
    return pl.pallas_call(
        functools.partial(_attn_wo_kernel, alpha),
        grid=(b, s // tq),
        in_specs=[
            pl.BlockSpec(memory_space=pltpu.SMEM),
            tile(ATTN_WIDTH),
            pl.BlockSpec((None, BLOCK, KV_WIDTH), lambda bi, i: (bi, prev(i), kcol)),
            pl.BlockSpec((None, tq, KV_WIDTH), lambda bi, i: (bi, i, kcol)),
            pl.BlockSpec((None, BLOCK, KV_WIDTH), lambda bi, i: (bi, nxt(i), kcol)),
            pl.BlockSpec((None, BLOCK, KV_WIDTH), lambda bi, i: (bi, prev(i), vcol)),
            pl.BlockSpec((None, tq, KV_WIDTH), lambda bi, i: (bi, i, vcol)),
            pl.BlockSpec((None, BLOCK, KV_WIDTH), lambda bi, i: (bi, nxt(i), vcol)),
            const(*pen.shape),
            tile(SGU_WIDTH),
            tile(d),
            const(ATTN_WIDTH + SGU_WIDTH, d),
            const(1, d),
            const(1, d),
        ],
        out_specs=tile(d),
        out_shape=jax.ShapeDtypeStruct((b, s, d), F32),
        compiler_params=pltpu.CompilerParams(
            dimension_semantics=("arbitrary", "arbitrary"), vmem_limit_bytes=VMEM_LIMIT),
        name="attn_wo",
    )(sink, qkv, qkv, qkv, qkv, qkv, qkv, qkv, pen, sgu, x, w_o, g1, b1)


def _ffn_kernel(alpha, x1_ref, wg_ref, wu_ref, wd_ref, g2_ref, b2_ref, out_ref, x1b_ref):
    j = pl.program_id(1)

    @pl.when(j == 0)
    def _():
        x1 = x1_ref[...]
        x1b_ref[...] = x1.astype(BF16)
        out_ref[...] = alpha * x1

    xb = x1b_ref[...]
    gate = jnp.dot(xb, wg_ref[...], preferred_element_type=F32)
    up = jnp.dot(xb, wu_ref[...], preferred_element_type=F32)
    act = (gate * _sigmoid(gate) * up).astype(BF16)
    out_ref[...] += jnp.dot(act, wd_ref[...], preferred_element_type=F32)

    @pl.when(j == pl.num_programs(1) - 1)
    def _():
        out_ref[...] = _layer_norm(out_ref[...], g2_ref[...], b2_ref[...])


def _ffn_call(alpha, x1, w_gate, w_up, w_down, g2, b2):
    t, d = x1.shape
    d_ff = w_gate.shape[1]
    tm, tf = FFN_TM, FFN_TF
    const = lambda *shape: pl.BlockSpec(shape, lambda i, j: (0,) * len(shape), pipeline_mode=pl.Buffered(1))
    return pl.pallas_call(
        functools.partial(_ffn_kernel, alpha),
        grid=(t // tm, d_ff // tf),
        in_specs=[
            pl.BlockSpec((tm, d), lambda i, j: (i, 0)),
            pl.BlockSpec((d, tf), lambda i, j: (0, j)),
            pl.BlockSpec((d, tf), lambda i, j: (0, j)),
            pl.BlockSpec((tf, d), lambda i, j: (j, 0)),
            const(1, d),
            const(1, d),
        ],
        out_specs=pl.BlockSpec((tm, d), lambda i, j: (i, 0)),
        out_shape=jax.ShapeDtypeStruct((t, d), F32),
        scratch_shapes=[pltpu.VMEM((tm, d), BF16)],
        compiler_params=pltpu.CompilerParams(
            dimension_semantics=("arbitrary", "arbitrary"), vmem_limit_bytes=FFN_VMEM_LIMIT),
        name="ffn",
    )(x1, w_gate, w_up, w_down, g2, b2)


def _layer(alpha, x, w_in, ln_sgu_g, ln_sgu_b, w_s, bs_full, sink, g1, b1, g2, b2, late_weights):
    b, s, d = x.shape
    if late_weights[0].dtype == BF16:
        qkv, sgu = _proj_call(x.reshape(b * s, d), w_in, ln_sgu_g, ln_sgu_b, w_s, bs_full)
    else:
        qkv, sgu, *late_weights = _proj_call(x.reshape(b * s, d), w_in, ln_sgu_g, ln_sgu_b, w_s, bs_full,
                                             cast=late_weights)
    w_o, w_gate, w_up, w_down = late_weights
    x1 = _attn_wo_call(alpha, qkv.reshape(b, s, QKV_WIDTH), sink, sgu.reshape(b, s, SGU_WIDTH), x, w_o, g1, b1)
    out = _ffn_call(alpha, x1.reshape(b * s, d), w_gate, w_up, w_down, g2, b2)
    return out.reshape(b, s, d), tuple(late_weights)


def kernel(x_prompt, x_sample, w_in, ln_sgu_g, ln_sgu_b, w_s, b_s, attn_sink, w_o, ln1_g, ln1_b,
           w_gate, w_up, w_down, ln2_g, ln2_b):
    depth = w_in.shape[0]
    alpha = (2.0 * depth) ** 0.25
    row = lambda v: v.reshape(1, -1).astype(F32)
    y_prompt, y_sample = x_prompt, x_sample
    for l in range(depth):
        bs_full = jnp.repeat(b_s[l].T.astype(F32), SGU_GROUP_DIM, axis=1)
        params = (
            w_in[l].astype(BF16), row(ln_sgu_g[l]), row(ln_sgu_b[l]), w_s[l].astype(BF16), bs_full,
            attn_sink[l].astype(F32), row(ln1_g[l]), row(ln1_b[l]), row(ln2_g[l]), row(ln2_b[l]),
        )
        late_f32 = tuple(w[l].astype(F32) for w in (w_o, w_gate, w_up, w_down))
        y_prompt, late_bf16 = _layer(alpha, y_prompt, *params, late_f32)
        y_sample, _ = _layer(alpha, y_sample, *params, late_bf16)
    return (y_prompt, y_sample)
```

```python
import functools
import math

import numpy as np
import jax
import jax.numpy as jnp
from jax import lax
from jax.experimental import pallas as pl
from jax.experimental.pallas import tpu as pltpu

HEAD_DIM = 128
N_HEADS = 8
N_KV_HEADS = 2
GQA = N_HEADS // N_KV_HEADS
ATTN_WIDTH = N_HEADS * HEAD_DIM
KV_WIDTH = N_KV_HEADS * HEAD_DIM
QKV_WIDTH = ATTN_WIDTH + 2 * KV_WIDTH
N_SGU_GROUPS = 8
SGU_GROUP_DIM = 128
SGU_WIDTH = N_SGU_GROUPS * SGU_GROUP_DIM
BLOCK = 128
WINDOW = 128
LN_EPS = 1e-5
NEG_INF = -1e30
LOG2E = math.log2(math.e)
BF16_SUBLANES = 16

PROJ_TM = 512
ATTN_TQ = 512
ATTN_WO_ROWS = 256
ATTN_WO_COLS = 512
FFN_TM = 1024
FFN_TF = 512
VMEM_LIMIT = 56 * 1024 * 1024
FFN_VMEM_LIMIT = 62 * 1024 * 1024

F32 = jnp.float32
BF16 = jnp.bfloat16


def _layer_norm(x, g, b):
    mu = jnp.mean(x, axis=-1, keepdims=True)
    xc = x - mu
    var = jnp.mean(xc * xc, axis=-1, keepdims=True)
    return xc * lax.rsqrt(var + LN_EPS) * g + b


def _gelu_tanh(x):
    c = math.sqrt(2.0 / math.pi)
    return x * (0.5 * (1.0 + jnp.tanh(c * (x + 0.044715 * (x * x * x)))))


def _sigmoid(x):
    return 0.5 * (jnp.tanh(0.5 * x) + 1.0)


def _proj_kernel(n_cast, x_ref, w_ref, lng_ref, lnb_ref, ws_ref, bs_ref, *refs):
    cast_in, (qkv_ref, sgu_ref), cast_out = refs[:n_cast], refs[n_cast:n_cast + 2], refs[n_cast + 2:]
    tm = x_ref.shape[0]
    xb = x_ref[...].astype(BF16)
    gv = _gelu_tanh(jnp.dot(xb, w_ref[:, QKV_WIDTH + SGU_WIDTH:], preferred_element_type=F32))
    gv = _layer_norm(gv, lng_ref[...], lnb_ref[...]).astype(BF16)
    u = _gelu_tanh(jnp.dot(xb, w_ref[:, QKV_WIDTH:QKV_WIDTH + SGU_WIDTH], preferred_element_type=F32))
    q = jnp.dot(xb, w_ref[:, :ATTN_WIDTH], preferred_element_type=F32)
    qkv_ref[:, :ATTN_WIDTH] = (q * (LOG2E / math.sqrt(HEAD_DIM))).astype(BF16)
    for c in range(tm // BLOCK):
        rows = slice(c * BLOCK, (c + 1) * BLOCK)
        for g in range(N_SGU_GROUPS):
            cols = slice(g * SGU_GROUP_DIM, (g + 1) * SGU_GROUP_DIM)
            mixed = jnp.dot(ws_ref[g], gv[rows, cols], preferred_element_type=F32) + bs_ref[:, cols]
            sgu_ref[rows, cols] = (u[rows, cols] * mixed).astype(BF16)
    kv = jnp.dot(xb, w_ref[:, ATTN_WIDTH:QKV_WIDTH], preferred_element_type=F32)
    qkv_ref[:, ATTN_WIDTH:] = kv.astype(BF16)
    for src, dst in zip(cast_in, cast_out):
        dst[...] = src[...].astype(BF16)


def _proj_call(x2, w_in, ln_g, ln_b, w_s, bs_full, cast=()):
    t, d = x2.shape
    tm = PROJ_TM
    steps = t // tm
    in_w = w_in.shape[1]
    const = lambda *shape: pl.BlockSpec(shape, lambda i: (0,) * len(shape), pipeline_mode=pl.Buffered(1))
    for w in cast:
        assert w.shape[0] % (steps * BF16_SUBLANES) == 0, (w.shape, steps)
    slab = lambda w: pl.BlockSpec((w.shape[0] // steps, w.shape[1]), lambda i: (i, 0))
    return pl.pallas_call(
        functools.partial(_proj_kernel, len(cast)),
        grid=(steps,),
        in_specs=[
            pl.BlockSpec((tm, d), lambda i: (i, 0)),
            const(d, in_w),
            const(1, SGU_WIDTH),
            const(1, SGU_WIDTH),
            const(N_SGU_GROUPS, BLOCK, BLOCK),
            const(BLOCK, SGU_WIDTH),
        ] + [slab(w) for w in cast],
        out_specs=[
            pl.BlockSpec((tm, QKV_WIDTH), lambda i: (i, 0)),
            pl.BlockSpec((tm, SGU_WIDTH), lambda i: (i, 0)),
        ] + [slab(w) for w in cast],
        out_shape=[
            jax.ShapeDtypeStruct((t, QKV_WIDTH), BF16),
            jax.ShapeDtypeStruct((t, SGU_WIDTH), BF16),
        ] + [jax.ShapeDtypeStruct(w.shape, BF16) for w in cast],
        compiler_params=pltpu.CompilerParams(
            dimension_semantics=("arbitrary",), vmem_limit_bytes=VMEM_LIMIT),
        name="proj_sgu",
    )(x2, w_in, ln_g, ln_b, w_s, bs_full, *cast)


def _attn_units(q_ref, kk, vv, pen_ref, sink_ref, units, edge_pen):
    scores = []
    for c, h in units:
        rows = slice(c * BLOCK, (c + 1) * BLOCK)
        qs = jnp.concatenate(
            [q_ref[rows, (h * GQA + g) * HEAD_DIM:(h * GQA + g + 1) * HEAD_DIM] for g in range(GQA)], axis=0)
        kc = kk[h][c * BLOCK:(c + 3) * BLOCK]
        s = lax.dot_general(qs, kc, (((1,), (1,)), ((), ())), preferred_element_type=F32)
        s = s + pen_ref[h * GQA * BLOCK:(h + 1) * GQA * BLOCK, :]
        for pen in edge_pen(c):
            s = s + pen
        scores.append(s)
    probs = []
    for (c, h), s in zip(units, scores):
        ps, rdens = [], []
        for g in range(GQA):
            sg = s[g * BLOCK:(g + 1) * BLOCK]
            sk = sink_ref[h * GQA + g] * LOG2E
            m = jnp.maximum(jnp.max(sg, axis=-1, keepdims=True), sk)
            p = jnp.exp2(sg - m)
            rdens.append(1.0 / (jnp.sum(p, axis=-1, keepdims=True) + jnp.exp2(sk - m)))
            ps.append(p.astype(BF16))
        probs.append((jnp.concatenate(ps, axis=0), rdens))
    out = {}
    for (c, h), (p, rdens) in zip(units, probs):
        o = jnp.dot(p, vv[h][c * BLOCK:(c + 3) * BLOCK], preferred_element_type=F32)
        for g in range(GQA):
            out[(c, h * GQA + g)] = (o[g * BLOCK:(g + 1) * BLOCK] * rdens[g]).astype(BF16)
    return out


def _attn_wo_kernel(alpha, sink_ref, q_ref, kp_ref, km_ref, kn_ref, vp_ref, vm_ref, vn_ref, pen_ref,
                    sgu_ref, x_ref, wo_ref, g1_ref, b1_ref, x1_ref):
    tq = q_ref.shape[0]
    d = x_ref.shape[1]
    nchunk = tq // BLOCK
    sub = ATTN_WO_ROWS // BLOCK
    nblk = nchunk // sub
    npiece = d // ATTN_WO_COLS
    i = pl.program_id(1)
    last = pl.num_programs(1) - 1
    col = lax.broadcasted_iota(jnp.int32, (1, 3 * BLOCK), 1)
    lo_pen = jnp.where((col < BLOCK) & (i == 0), NEG_INF, 0.0).astype(F32)
    hi_pen = jnp.where((col >= 2 * BLOCK) & (i == last), NEG_INF, 0.0).astype(F32)
    edge_pen = lambda c: ([lo_pen] if c == 0 else []) + ([hi_pen] if c == nchunk - 1 else [])
    kk, vv = [], []
    for h in range(N_KV_HEADS):
        hc = slice(h * HEAD_DIM, (h + 1) * HEAD_DIM)
        kk.append(jnp.concatenate([kp_ref[:, hc], km_ref[:, hc], kn_ref[:, hc]], axis=0))
        vv.append(jnp.concatenate([vp_ref[:, hc], vm_ref[:, hc], vn_ref[:, hc]], axis=0))

    heads = {}

    def attend(chunks):
        units = [(c, h) for c in chunks for h in range(N_KV_HEADS)]
        heads.update(_attn_units(q_ref, kk, vv, pen_ref, sink_ref, units, edge_pen))

    def wo_piece(blk, attn, n):
        rows = slice(blk * ATTN_WO_ROWS, (blk + 1) * ATTN_WO_ROWS)
        cols = slice(n * ATTN_WO_COLS, (n + 1) * ATTN_WO_COLS)
        piece = jnp.dot(attn, wo_ref[:ATTN_WIDTH, cols], preferred_element_type=F32)
        return piece + jnp.dot(sgu_ref[rows, :], wo_ref[ATTN_WIDTH:, cols], preferred_element_type=F32)

    def norm_store(blk, pieces):
        rows = slice(blk * ATTN_WO_ROWS, (blk + 1) * ATTN_WO_ROWS)
        mix = jnp.concatenate(pieces, axis=1)
        x1_ref[rows, :] = _layer_norm(alpha * x_ref[rows, :] + mix, g1_ref[...], b1_ref[...])

    for c in range(sub):
        attend([c])
    prev_pieces = None
    for blk in range(nblk):
        attn = jnp.concatenate(
            [jnp.concatenate([heads[(c, hd)] for hd in range(N_HEADS)], axis=1)
             for c in range(blk * sub, (blk + 1) * sub)], axis=0)
        pending = list(range((blk + 1) * sub, (blk + 2) * sub)) if blk + 1 < nblk else []
        pieces = []
        for n in range(npiece):
            pieces.append(wo_piece(blk, attn, n))
            if n % 2 == 0 and n // 2 < len(pending):
                attend([pending[n // 2]])
            if n == 1 and prev_pieces is not None:
                norm_store(blk - 1, prev_pieces)
        prev_pieces = pieces
    norm_store(nblk - 1, prev_pieces)


def _alibi_penalty():
    slopes = 2.0 ** (-8.0 * np.arange(1, N_HEADS + 1, dtype=np.float64) / N_HEADS)
    rel = np.arange(3 * BLOCK)[None, :] - BLOCK - np.arange(BLOCK)[:, None]
    dist = np.abs(rel).astype(np.float64)
    pen = np.where(dist[None] <= WINDOW, -slopes[:, None, None] * dist[None] * LOG2E, NEG_INF)
    return pen.reshape(N_HEADS * BLOCK, 3 * BLOCK).astype(np.float32)


def _attn_wo_call(alpha, qkv, sink, sgu, x, w_o, g1, b1):
    b, s, d = x.shape
    tq = ATTN_TQ
    r = tq // BLOCK
    nc = s // BLOCK
    kcol = ATTN_WIDTH // KV_WIDTH
    vcol = kcol + 1
    prev = lambda i: jnp.maximum(i * r - 1, 0)
    nxt = lambda i: jnp.minimum((i + 1) * r, nc - 1)
    pen = jnp.asarray(_alibi_penalty())
    const = lambda *shape: pl.BlockSpec(shape, lambda bi, i: (0,) * len(shape), pipeline_mode=pl.Buffered(1))
    tile = lambda width: pl.BlockSpec((None, tq, width), lambda bi, i: (bi, i, 0))
    return pl.pallas_call(
        functools.partial(_attn_wo_kernel, alpha),
        grid=(b, s // tq),
        in_specs=[
            pl.BlockSpec(memory_space=pltpu.SMEM),
            tile(ATTN_WIDTH),
            pl.BlockSpec((None, BLOCK, KV_WIDTH), lambda bi, i: (bi, prev(i), kcol)),
            pl.BlockSpec((None, tq, KV_WIDTH), lambda bi, i: (bi, i, kcol)),
            pl.BlockSpec((None, BLOCK, KV_WIDTH), lambda bi, i: (bi, nxt(i), kcol)),
            pl.BlockSpec((None, BLOCK, KV_WIDTH), lambda bi, i: (bi, prev(i), vcol)),
            pl.BlockSpec((None, tq, KV_WIDTH), lambda bi, i: (bi, i, vcol)),
            pl.BlockSpec((None, BLOCK, KV_WIDTH), lambda bi, i: (bi, nxt(i), vcol)),
            const(*pen.shape),
            tile(SGU_WIDTH),
            tile(d),
            const(ATTN_WIDTH + SGU_WIDTH, d),
            const(1, d),
            const(1, d),
        ],
        out_specs=tile(d),
        out_shape=jax.ShapeDtypeStruct((b, s, d), F32),
        compiler_params=pltpu.CompilerParams(
            dimension_semantics=("arbitrary", "arbitrary"), vmem_limit_bytes=VMEM_LIMIT),
        name="attn_wo",
    )(sink, qkv, qkv, qkv, qkv, qkv, qkv, qkv, pen, sgu, x, w_o, g1, b1)


def _ffn_kernel(alpha, x1_ref, wg_ref, wu_ref, wd_ref, g2_ref, b2_ref, out_ref, x1b_ref):
    j = pl.program_id(1)

    @pl.when(j == 0)
    def _():
        x1 = x1_ref[...]
        x1b_ref[...] = x1.astype(BF16)
        out_ref[...] = alpha * x1

    xb = x1b_ref[...]
    gate = jnp.dot(xb, wg_ref[...], preferred_element_type=F32)
    up = jnp.dot(xb, wu_ref[...], preferred_element_type=F32)
    act = (gate * _sigmoid(gate) * up).astype(BF16)
    out_ref[...] += jnp.dot(act, wd_ref[...], preferred_element_type=F32)

    @pl.when(j == pl.num_programs(1) - 1)
    def _():
        out_ref[...] = _layer_norm(out_ref[...], g2_ref[...], b2_ref[...])


def _ffn_call(alpha, x1, w_gate, w_up, w_down, g2, b2):
    t, d = x1.shape
    d_ff = w_gate.shape[1]
    tm, tf = FFN_TM, FFN_TF
    const = lambda *shape: pl.BlockSpec(shape, lambda i, j: (0,) * len(shape), pipeline_mode=pl.Buffered(1))
    return pl.pallas_call(
        functools.partial(_ffn_kernel, alpha),
        grid=(t // tm, d_ff // tf),
        in_specs=[
            pl.BlockSpec((tm, d), lambda i, j: (i, 0)),
            pl.BlockSpec((d, tf), lambda i, j: (0, j)),
            pl.BlockSpec((d, tf), lambda i, j: (0, j)),
            pl.BlockSpec((tf, d), lambda i, j: (j, 0)),
            const(1, d),
            const(1, d),
        ],
        out_specs=pl.BlockSpec((tm, d), lambda i, j: (i, 0)),
        out_shape=jax.ShapeDtypeStruct((t, d), F32),
        scratch_shapes=[pltpu.VMEM((tm, d), BF16)],
        compiler_params=pltpu.CompilerParams(
            dimension_semantics=("arbitrary", "arbitrary"), vmem_limit_bytes=FFN_VMEM_LIMIT),
        name="ffn",
    )(x1, w_gate, w_up, w_down, g2, b2)


def _layer(alpha, x, w_in, ln_sgu_g, ln_sgu_b, w_s, bs_full, sink, g1, b1, g2, b2, late_weights):
    b, s, d = x.shape
    if late_weights[0].dtype == BF16:
        qkv, sgu = _proj_call(x.reshape(b * s, d), w_in, ln_sgu_g, ln_sgu_b, w_s, bs_full)
    else:
        qkv, sgu, *late_weights = _proj_call(x.reshape(b * s, d), w_in, ln_sgu_g, ln_sgu_b, w_s, bs_full,
                                             cast=late_weights)
    w_o, w_gate, w_up, w_down = late_weights
    x1 = _attn_wo_call(alpha, qkv.reshape(b, s, QKV_WIDTH), sink, sgu.reshape(b, s, SGU_WIDTH), x, w_o, g1, b1)
    out = _ffn_call(alpha, x1.reshape(b * s, d), w_gate, w_up, w_down, g2, b2)
    return out.reshape(b, s, d), tuple(late_weights)


def kernel(x_prompt, x_sample, w_in, ln_sgu_g, ln_sgu_b, w_s, b_s, attn_sink, w_o, ln1_g, ln1_b,
           w_gate, w_up, w_down, ln2_g, ln2_b):
    depth = w_in.shape[0]
    alpha = (2.0 * depth) ** 0.25
    row = lambda v: v.reshape(1, -1).astype(F32)
    y_prompt, y_sample = x_prompt, x_sample
    for l in range(depth):
        bs_full = jnp.repeat(b_s[l].T.astype(F32), SGU_GROUP_DIM, axis=1)
        params = (
            w_in[l].astype(BF16), row(ln_sgu_g[l]), row(ln_sgu_b[l]), w_s[l].astype(BF16), bs_full,
            attn_sink[l].astype(F32), row(ln1_g[l]), row(ln1_b[l]), row(ln2_g[l]), row(ln2_b[l]),
        )
        late_f32 = tuple(w[l].astype(F32) for w in (w_o, w_gate, w_up, w_down))
        y_prompt, late_bf16 = _layer(alpha, y_prompt, *params, late_f32)
        y_sample, _ = _layer(alpha, y_sample, *params, late_bf16)
    return (y_prompt, y_sample)
```

```python
import functools
import math

import numpy as np
import jax
import jax.numpy as jnp
from jax import lax
from jax.experimental import pallas as pl
from jax.experimental.pallas import tpu as pltpu

HEAD_DIM = 128
N_HEADS = 8
N_KV_HEADS = 2
GQA = N_HEADS // N_KV_HEADS
ATTN_WIDTH = N_HEADS * HEAD_DIM
KV_WIDTH = N_KV_HEADS * HEAD_DIM
QKV_WIDTH = ATTN_WIDTH + 2 * KV_WIDTH
N_SGU_GROUPS = 8
SGU_GROUP_DIM = 128
SGU_WIDTH = N_SGU_GROUPS * SGU_GROUP_DIM
BLOCK = 128
WINDOW = 128
LN_EPS = 1e-5
NEG_INF = -1e30
LOG2E = math.log2(math.e)
BF16_SUBLANES = 16

PROJ_TM = 512
ATTN_TQ = 512
ATTN_WO_ROWS = 256
ATTN_WO_COLS = 512
FFN_TM = 1024
FFN_TF = 512
VMEM_LIMIT = 56 * 1024 * 1024
FFN_VMEM_LIMIT = 62 * 1024 * 1024

F32 = jnp.float32
BF16 = jnp.bfloat16


def _layer_norm(x, g, b):
    mu = jnp.mean(x, axis=-1, keepdims=True)
    xc = x - mu
    var = jnp.mean(xc * xc, axis=-1, keepdims=True)
    return xc * lax.rsqrt(var + LN_EPS) * g + b


def _gelu_tanh(x):
    c = math.sqrt(2.0 / math.pi)
    return x * (0.5 * (1.0 + jnp.tanh(c * (x + 0.044715 * (x * x * x)))))


def _sigmoid(x):
    return 0.5 * (jnp.tanh(0.5 * x) + 1.0)


def _proj_kernel(n_cast, x_ref, w_ref, lng_ref, lnb_ref, ws_ref, bs_ref, *refs):
    cast_in, (qkv_ref, sgu_ref), cast_out = refs[:n_cast], refs[n_cast:n_cast + 2], refs[n_cast + 2:]
    tm = x_ref.shape[0]
    xb = x_ref[...].astype(BF16)
    gv = _gelu_tanh(jnp.dot(xb, w_ref[:, QKV_WIDTH + SGU_WIDTH:], preferred_element_type=F32))
    gv = _layer_norm(gv, lng_ref[...], lnb_ref[...]).astype(BF16)
    u = _gelu_tanh(jnp.dot(xb, w_ref[:, QKV_WIDTH:QKV_WIDTH + SGU_WIDTH], preferred_element_type=F32))
    q = jnp.dot(xb, w_ref[:, :ATTN_WIDTH], preferred_element_type=F32)
    qkv_ref[:, :ATTN_WIDTH] = (q * (LOG2E / math.sqrt(HEAD_DIM))).astype(BF16)
    for c in range(tm // BLOCK):
        rows = slice(c * BLOCK, (c + 1) * BLOCK)
        for g in range(N_SGU_GROUPS):
            cols = slice(g * SGU_GROUP_DIM, (g + 1) * SGU_GROUP_DIM)
            mixed = jnp.dot(ws_ref[g], gv[rows, cols], preferred_element_type=F32) + bs_ref[:, cols]
            sgu_ref[rows, cols] = (u[rows, cols] * mixed).astype(BF16)
    kv = jnp.dot(xb, w_ref[:, ATTN_WIDTH:QKV_WIDTH], preferred_element_type=F32)
    qkv_ref[:, ATTN_WIDTH:] = kv.astype(BF16)
    for src, dst in zip(cast_in, cast_out):
        if len(dst.shape) == 2:
            dst[...] = src[...].astype(BF16)
        else:
            width = dst.shape[2]
            for c in range(dst.shape[0]):
                dst[c] = src[:, c * width:(c + 1) * width].astype(BF16)


def _proj_call(x2, w_in, ln_g, ln_b, w_s, bs_full, cast=(), chunk_cols=()):
    t, d = x2.shape
    tm = PROJ_TM
    steps = t // tm
    in_w = w_in.shape[1]
    const = lambda *shape: pl.BlockSpec(shape, lambda i: (0,) * len(shape), pipeline_mode=pl.Buffered(1))
    for w in cast:
        assert w.shape[0] % (steps * BF16_SUBLANES) == 0, (w.shape, steps)
    slab = lambda w: pl.BlockSpec((w.shape[0] // steps, w.shape[1]), lambda i: (i, 0))
    cast_specs, cast_shapes = [], []
    for w, cw in zip(cast, chunk_cols):
        if cw is None:
            cast_specs.append(slab(w))
            cast_shapes.append(jax.ShapeDtypeStruct(w.shape, BF16))
        else:
            cast_specs.append(pl.BlockSpec((w.shape[1] // cw, w.shape[0] // steps, cw), lambda i: (0, i, 0)))
            cast_shapes.append(jax.ShapeDtypeStruct((w.shape[1] // cw, w.shape[0], cw), BF16))
    return pl.pallas_call(
        functools.partial(_proj_kernel, len(cast)),
        grid=(steps,),
        in_specs=[
            pl.BlockSpec((tm, d), lambda i: (i, 0)),
            const(d, in_w),
            const(1, SGU_WIDTH),
            const(1, SGU_WIDTH),
            const(N_SGU_GROUPS, BLOCK, BLOCK),
            const(BLOCK, SGU_WIDTH),
        ] + [slab(w) for w in cast],
        out_specs=[
            pl.BlockSpec((tm, QKV_WIDTH), lambda i: (i, 0)),
            pl.BlockSpec((tm, SGU_WIDTH), lambda i: (i, 0)),
        ] + cast_specs,
        out_shape=[
            jax.ShapeDtypeStruct((t, QKV_WIDTH), BF16),
            jax.ShapeDtypeStruct((t, SGU_WIDTH), BF16),
        ] + cast_shapes,
        compiler_params=pltpu.CompilerParams(
            dimension_semantics=("arbitrary",), vmem_limit_bytes=VMEM_LIMIT),
        name="proj_sgu",
    )(x2, w_in, ln_g, ln_b, w_s, bs_full, *cast)


def _attn_units(q_ref, kk, vv, pen_ref, sink_ref, units, edge_pen):
    scores = []
    for c, h in units:
        rows = slice(c * BLOCK, (c + 1) * BLOCK)
        qs = jnp.concatenate(
            [q_ref[rows, (h * GQA + g) * HEAD_DIM:(h * GQA + g + 1) * HEAD_DIM] for g in range(GQA)], axis=0)
        kc = kk[h][c * BLOCK:(c + 3) * BLOCK]
        s = lax.dot_general(qs, kc, (((1,), (1,)), ((), ())), preferred_element_type=F32)
        s = s + pen_ref[h * GQA * BLOCK:(h + 1) * GQA * BLOCK, :]
        for pen in edge_pen(c):
            s = s + pen
        scores.append(s)
    probs = []
    for (c, h), s in zip(units, scores):
        ps, rdens = [], []
        for g in range(GQA):
            sg = s[g * BLOCK:(g + 1) * BLOCK]
            sk = sink_ref[h * GQA + g] * LOG2E
            m = jnp.maximum(jnp.max(sg, axis=-1, keepdims=True), sk)
            p = jnp.exp2(sg - m)
            rdens.append(1.0 / (jnp.sum(p, axis=-1, keepdims=True) + jnp.exp2(sk - m)))
            ps.append(p.astype(BF16))
        probs.append((jnp.concatenate(ps, axis=0), rdens))
    out = {}
    for (c, h), (p, rdens) in zip(units, probs):
        o = jnp.dot(p, vv[h][c * BLOCK:(c + 3) * BLOCK], preferred_element_type=F32)
        for g in range(GQA):
            out[(c, h * GQA + g)] = (o[g * BLOCK:(g + 1) * BLOCK] * rdens[g]).astype(BF16)
    return out


def _attn_wo_kernel(alpha, sink_ref, q_ref, kp_ref, km_ref, kn_ref, vp_ref, vm_ref, vn_ref, pen_ref,
                    sgu_ref, x_ref, wo_ref, g1_ref, b1_ref, x1_ref):
    tq = q_ref.shape[0]
    d = x_ref.shape[1]
    nchunk = tq // BLOCK
    sub = ATTN_WO_ROWS // BLOCK
    nblk = nchunk // sub
    npiece = d // ATTN_WO_COLS
    i = pl.program_id(1)
    last = pl.num_programs(1) - 1
    col = lax.broadcasted_iota(jnp.int32, (1, 3 * BLOCK), 1)
    lo_pen = jnp.where((col < BLOCK) & (i == 0), NEG_INF, 0.0).astype(F32)
    hi_pen = jnp.where((col >= 2 * BLOCK) & (i == last), NEG_INF, 0.0).astype(F32)
    edge_pen = lambda c: ([lo_pen] if c == 0 else []) + ([hi_pen] if c == nchunk - 1 else [])
    kk, vv = [], []
    for h in range(N_KV_HEADS):
        hc = slice(h * HEAD_DIM, (h + 1) * HEAD_DIM)
        kk.append(jnp.concatenate([kp_ref[:, hc], km_ref[:, hc], kn_ref[:, hc]], axis=0))
        vv.append(jnp.concatenate([vp_ref[:, hc], vm_ref[:, hc], vn_ref[:, hc]], axis=0))

    heads = {}

    def attend(chunks):
        units = [(c, h) for c in chunks for h in range(N_KV_HEADS)]
        heads.update(_attn_units(q_ref, kk, vv, pen_ref, sink_ref, units, edge_pen))

    def wo_piece(blk, attn, n):
        rows = slice(blk * ATTN_WO_ROWS, (blk + 1) * ATTN_WO_ROWS)
        cols = slice(n * ATTN_WO_COLS, (n + 1) * ATTN_WO_COLS)
        piece = jnp.dot(attn, wo_ref[:ATTN_WIDTH, cols], preferred_element_type=F32)
        return piece + jnp.dot(sgu_ref[rows, :], wo_ref[ATTN_WIDTH:, cols], preferred_element_type=F32)

    def norm_store(blk, pieces):
        rows = slice(blk * ATTN_WO_ROWS, (blk + 1) * ATTN_WO_ROWS)
        mix = jnp.concatenate(pieces, axis=1)
        x1_ref[rows, :] = _layer_norm(alpha * x_ref[rows, :] + mix, g1_ref[...], b1_ref[...])

    for c in range(sub):
        attend([c])
    prev_pieces = None
    for blk in range(nblk):
        attn = jnp.concatenate(
            [jnp.concatenate([heads[(c, hd)] for hd in range(N_HEADS)], axis=1)
             for c in range(blk * sub, (blk + 1) * sub)], axis=0)
        pending = list(range((blk + 1) * sub, (blk + 2) * sub)) if blk + 1 < nblk else []
        pieces = []
        for n in range(npiece):
            pieces.append(wo_piece(blk, attn, n))
            if n % 2 == 0 and n // 2 < len(pending):
                attend([pending[n // 2]])
            if n == 1 and prev_pieces is not None:
                norm_store(blk - 1, prev_pieces)
        prev_pieces = pieces
    norm_store(nblk - 1, prev_pieces)


def _alibi_penalty():
    slopes = 2.0 ** (-8.0 * np.arange(1, N_HEADS + 1, dtype=np.float64) / N_HEADS)
    rel = np.arange(3 * BLOCK)[None, :] - BLOCK - np.arange(BLOCK)[:, None]
    dist = np.abs(rel).astype(np.float64)
    pen = np.where(dist[None] <= WINDOW, -slopes[:, None, None] * dist[None] * LOG2E, NEG_INF)
    return pen.reshape(N_HEADS * BLOCK, 3 * BLOCK).astype(np.float32)


def _attn_wo_call(alpha, qkv, sink, sgu, x, w_o, g1, b1):
    b, s, d = x.shape
    tq = ATTN_TQ
    r = tq // BLOCK
    nc = s // BLOCK
    kcol = ATTN_WIDTH // KV_WIDTH
    vcol = kcol + 1
    prev = lambda i: jnp.maximum(i * r - 1, 0)
    nxt = lambda i: jnp.minimum((i + 1) * r, nc - 1)
    pen = jnp.asarray(_alibi_penalty())
    const = lambda *shape: pl.BlockSpec(shape, lambda bi, i: (0,) * len(shape), pipeline_mode=pl.Buffered(1))
    tile = lambda width: pl.BlockSpec((None, tq, width), lambda bi, i: (bi, i, 0))
    return pl.pallas_call(
        functools.partial(_attn_wo_kernel, alpha),
        grid=(b, s // tq),
        in_specs=[
            pl.BlockSpec(memory_space=pltpu.SMEM),
            tile(ATTN_WIDTH),
            pl.BlockSpec((None, BLOCK, KV_WIDTH), lambda bi, i: (bi, prev(i), kcol)),
            pl.BlockSpec((None, tq, KV_WIDTH), lambda bi, i: (bi, i, kcol)),
            pl.BlockSpec((None, BLOCK, KV_WIDTH), lambda bi, i: (bi, nxt(i), kcol)),
            pl.BlockSpec((None, BLOCK, KV_WIDTH), lambda bi, i: (bi, prev(i), vcol)),
            pl.BlockSpec((None, tq, KV_WIDTH), lambda bi, i: (bi, i, vcol)),
            pl.BlockSpec((None, BLOCK, KV_WIDTH), lambda bi, i: (bi, nxt(i), vcol)),
            const(*pen.shape),
            tile(SGU_WIDTH),
            tile(d),
            const(ATTN_WIDTH + SGU_WIDTH, d),
            const(1, d),
            const(1, d),
        ],
        out_specs=tile(d),
        out_shape=jax.ShapeDtypeStruct((b, s, d), F32),
        compiler_params=pltpu.CompilerParams(
            dimension_semantics=("arbitrary", "arbitrary"), vmem_limit_bytes=VMEM_LIMIT),
        name="attn_wo",
    )(sink, qkv, qkv, qkv, qkv, qkv, qkv, qkv, pen, sgu, x, w_o, g1, b1)


def _ffn_kernel(alpha, x1_ref, wg_ref, wu_ref, wd_ref, g2_ref, b2_ref, out_ref, x1b_ref):
    j = pl.program_id(1)

    @pl.when(j == 0)
    def _():
        x1 = x1_ref[...]
        x1b_ref[...] = x1.astype(BF16)
        out_ref[...] = alpha * x1

    xb = x1b_ref[...]
    gate = jnp.dot(xb, wg_ref[...], preferred_element_type=F32)
    up = jnp.dot(xb, wu_ref[...], preferred_element_type=F32)
    act = (gate * _sigmoid(gate) * up).astype(BF16)
    out_ref[...] += jnp.dot(act, wd_ref[...], preferred_element_type=F32)

    @pl.when(j == pl.num_programs(1) - 1)
    def _():
        out_ref[...] = _layer_norm(out_ref[...], g2_ref[...], b2_ref[...])


def _ffn_call(alpha, x1, w_gate, w_up, w_down, g2, b2):
    t, d = x1.shape
    nchunk, _, tf = w_gate.shape
    tm = FFN_TM
    const = lambda *shape: pl.BlockSpec(shape, lambda i, j: (0,) * len(shape), pipeline_mode=pl.Buffered(1))
    return pl.pallas_call(
        functools.partial(_ffn_kernel, alpha),
        grid=(t // tm, nchunk),
        in_specs=[
            pl.BlockSpec((tm, d), lambda i, j: (i, 0)),
            pl.BlockSpec((None, d, tf), lambda i, j: (j, 0, 0)),
            pl.BlockSpec((None, d, tf), lambda i, j: (j, 0, 0)),
            pl.BlockSpec((tf, d), lambda i, j: (j, 0)),
            const(1, d),
            const(1, d),
        ],
        out_specs=pl.BlockSpec((tm, d), lambda i, j: (i, 0)),
        out_shape=jax.ShapeDtypeStruct((t, d), F32),
        scratch_shapes=[pltpu.VMEM((tm, d), BF16)],
        compiler_params=pltpu.CompilerParams(
            dimension_semantics=("arbitrary", "arbitrary"), vmem_limit_bytes=FFN_VMEM_LIMIT),
        name="ffn",
    )(x1, w_gate, w_up, w_down, g2, b2)


def _layer(alpha, x, w_in, ln_sgu_g, ln_sgu_b, w_s, bs_full, sink, g1, b1, g2, b2, late_weights):
    b, s, d = x.shape
    if late_weights[0].dtype == BF16:
        qkv, sgu = _proj_call(x.reshape(b * s, d), w_in, ln_sgu_g, ln_sgu_b, w_s, bs_full)
    else:
        qkv, sgu, *late_weights = _proj_call(x.reshape(b * s, d), w_in, ln_sgu_g, ln_sgu_b, w_s, bs_full,
                                             cast=late_weights, chunk_cols=(None, FFN_TF, FFN_TF, None))
    w_o, w_gate, w_up, w_down = late_weights
    x1 = _attn_wo_call(alpha, qkv.reshape(b, s, QKV_WIDTH), sink, sgu.reshape(b, s, SGU_WIDTH), x, w_o, g1, b1)
    out = _ffn_call(alpha, x1.reshape(b * s, d), w_gate, w_up, w_down, g2, b2)
    return out.reshape(b, s, d), tuple(late_weights)


def kernel(x_prompt, x_sample, w_in, ln_sgu_g, ln_sgu_b, w_s, b_s, attn_sink, w_o, ln1_g, ln1_b,
           w_gate, w_up, w_down, ln2_g, ln2_b):
    depth = w_in.shape[0]
    alpha = (2.0 * depth) ** 0.25
    row = lambda v: v.reshape(1, -1).astype(F32)
    y_prompt, y_sample = x_prompt, x_sample
    for l in range(depth):
        bs_full = jnp.repeat(b_s[l].T.astype(F32), SGU_GROUP_DIM, axis=1)
        params = (
            w_in[l].astype(BF16), row(ln_sgu_g[l]), row(ln_sgu_b[l]), w_s[l].astype(BF16), bs_full,
            attn_sink[l].astype(F32), row(ln1_g[l]), row(ln1_b[l]), row(ln2_g[l]), row(ln2_b[l]),
        )
        late_f32 = tuple(w[l].astype(F32) for w in (w_o, w_gate, w_up, w_down))
        y_prompt, late_bf16 = _layer(alpha, y_prompt, *params, late_f32)
        y_sample, _ = _layer(alpha, y_sample, *params, late_bf16)
    return (y_prompt, y_sample)
```

```python
import functools
import math

import numpy as np
import jax
import jax.numpy as jnp
from jax import lax
from jax.experimental import pallas as pl
from jax.experimental.pallas import tpu as pltpu

HEAD_DIM = 128
N_HEADS = 8
N_KV_HEADS = 2
GQA = N_HEADS // N_KV_HEADS
ATTN_WIDTH = N_HEADS * HEAD_DIM
KV_WIDTH = N_KV_HEADS * HEAD_DIM
QKV_WIDTH = ATTN_WIDTH + 2 * KV_WIDTH
N_SGU_GROUPS = 8
SGU_GROUP_DIM = 128
SGU_WIDTH = N_SGU_GROUPS * SGU_GROUP_DIM
BLOCK = 128
WINDOW = 128
LN_EPS = 1e-5
NEG_INF = -1e30
LOG2E = math.log2(math.e)
BF16_SUBLANES = 16

PROJ_TM = 512
ATTN_TQ = 512
ATTN_WO_ROWS = 256
ATTN_WO_COLS = 512
FFN_TM = 1024
FFN_TF = 512
FFN_LN_ROWS = 256
VMEM_LIMIT = 56 * 1024 * 1024
FFN_VMEM_LIMIT = 62 * 1024 * 1024

F32 = jnp.float32
BF16 = jnp.bfloat16


def _layer_norm(x, g, b):
    mu = jnp.mean(x, axis=-1, keepdims=True)
    xc = x - mu
    var = jnp.mean(xc * xc, axis=-1, keepdims=True)
    return xc * lax.rsqrt(var + LN_EPS) * g + b


def _gelu_tanh(x):
    c = math.sqrt(2.0 / math.pi)
    return x * (0.5 * (1.0 + jnp.tanh(c * (x + 0.044715 * (x * x * x)))))


def _sigmoid(x):
    return 0.5 * (jnp.tanh(0.5 * x) + 1.0)


def _proj_kernel(n_cast, x_ref, w_ref, lng_ref, lnb_ref, ws_ref, bs_ref, *refs):
    cast_in, (qkv_ref, sgu_ref), cast_out = refs[:n_cast], refs[n_cast:n_cast + 2], refs[n_cast + 2:]
    tm = x_ref.shape[0]
    xb = x_ref[...].astype(BF16)
    gv = _gelu_tanh(jnp.dot(xb, w_ref[:, QKV_WIDTH + SGU_WIDTH:], preferred_element_type=F32))
    gv = _layer_norm(gv, lng_ref[...], lnb_ref[...]).astype(BF16)
    u = _gelu_tanh(jnp.dot(xb, w_ref[:, QKV_WIDTH:QKV_WIDTH + SGU_WIDTH], preferred_element_type=F32))
    q = jnp.dot(xb, w_ref[:, :ATTN_WIDTH], preferred_element_type=F32)
    qkv_ref[:, :ATTN_WIDTH] = (q * (LOG2E / math.sqrt(HEAD_DIM))).astype(BF16)
    for c in range(tm // BLOCK):
        rows = slice(c * BLOCK, (c + 1) * BLOCK)
        for g in range(N_SGU_GROUPS):
            cols = slice(g * SGU_GROUP_DIM, (g + 1) * SGU_GROUP_DIM)
            mixed = jnp.dot(ws_ref[g], gv[rows, cols], preferred_element_type=F32) + bs_ref[:, cols]
            sgu_ref[rows, cols] = (u[rows, cols] * mixed).astype(BF16)
    kv = jnp.dot(xb, w_ref[:, ATTN_WIDTH:QKV_WIDTH], preferred_element_type=F32)
    qkv_ref[:, ATTN_WIDTH:] = kv.astype(BF16)
    for src, dst in zip(cast_in, cast_out):
        if len(dst.shape) == 2:
            dst[...] = src[...].astype(BF16)
        else:
            width = dst.shape[2]
            for c in range(dst.shape[0]):
                dst[c] = src[:, c * width:(c + 1) * width].astype(BF16)


def _proj_call(x2, w_in, ln_g, ln_b, w_s, bs_full, cast=(), chunk_cols=()):
    t, d = x2.shape
    tm = PROJ_TM
    steps = t // tm
    in_w = w_in.shape[1]
    const = lambda *shape: pl.BlockSpec(shape, lambda i: (0,) * len(shape), pipeline_mode=pl.Buffered(1))
    for w in cast:
        assert w.shape[0] % (steps * BF16_SUBLANES) == 0, (w.shape, steps)
    slab = lambda w: pl.BlockSpec((w.shape[0] // steps, w.shape[1]), lambda i: (i, 0))
    cast_specs, cast_shapes = [], []
    for w, cw in zip(cast, chunk_cols):
        if cw is None:
            cast_specs.append(slab(w))
            cast_shapes.append(jax.ShapeDtypeStruct(w.shape, BF16))
        else:
            cast_specs.append(pl.BlockSpec((w.shape[1] // cw, w.shape[0] // steps, cw), lambda i: (0, i, 0)))
            cast_shapes.append(jax.ShapeDtypeStruct((w.shape[1] // cw, w.shape[0], cw), BF16))
    return pl.pallas_call(
        functools.partial(_proj_kernel, len(cast)),
        grid=(steps,),
        in_specs=[
            pl.BlockSpec((tm, d), lambda i: (i, 0)),
            const(d, in_w),
            const(1, SGU_WIDTH),
            const(1, SGU_WIDTH),
            const(N_SGU_GROUPS, BLOCK, BLOCK),
            const(BLOCK, SGU_WIDTH),
        ] + [slab(w) for w in cast],
        out_specs=[
            pl.BlockSpec((tm, QKV_WIDTH), lambda i: (i, 0)),
            pl.BlockSpec((tm, SGU_WIDTH), lambda i: (i, 0)),
        ] + cast_specs,
        out_shape=[
            jax.ShapeDtypeStruct((t, QKV_WIDTH), BF16),
            jax.ShapeDtypeStruct((t, SGU_WIDTH), BF16),
        ] + cast_shapes,
        compiler_params=pltpu.CompilerParams(
            dimension_semantics=("arbitrary",), vmem_limit_bytes=VMEM_LIMIT),
        name="proj_sgu",
    )(x2, w_in, ln_g, ln_b, w_s, bs_full, *cast)


def _attn_units(q_ref, kk, vv, pen_ref, sink_ref, units, edge_pen):
    scores = []
    for c, h in units:
        rows = slice(c * BLOCK, (c + 1) * BLOCK)
        qs = jnp.concatenate(
            [q_ref[rows, (h * GQA + g) * HEAD_DIM:(h * GQA + g + 1) * HEAD_DIM] for g in range(GQA)], axis=0)
        kc = kk[h][c * BLOCK:(c + 3) * BLOCK]
        s = lax.dot_general(qs, kc, (((1,), (1,)), ((), ())), preferred_element_type=F32)
        s = s + pen_ref[h * GQA * BLOCK:(h + 1) * GQA * BLOCK, :]
        for pen in edge_pen(c):
            s = s + pen
        scores.append(s)
    probs = []
    for (c, h), s in zip(units, scores):
        ps, rdens = [], []
        for g in range(GQA):
            sg = s[g * BLOCK:(g + 1) * BLOCK]
            sk = sink_ref[h * GQA + g] * LOG2E
            m = jnp.maximum(jnp.max(sg, axis=-1, keepdims=True), sk)
            p = jnp.exp2(sg - m)
            rdens.append(1.0 / (jnp.sum(p, axis=-1, keepdims=True) + jnp.exp2(sk - m)))
            ps.append(p.astype(BF16))
        probs.append((jnp.concatenate(ps, axis=0), rdens))
    out = {}
    for (c, h), (p, rdens) in zip(units, probs):
        o = jnp.dot(p, vv[h][c * BLOCK:(c + 3) * BLOCK], preferred_element_type=F32)
        for g in range(GQA):
            out[(c, h * GQA + g)] = (o[g * BLOCK:(g + 1) * BLOCK] * rdens[g]).astype(BF16)
    return out


def _attn_wo_kernel(alpha, sink_ref, q_ref, kp_ref, km_ref, kn_ref, vp_ref, vm_ref, vn_ref, pen_ref,
                    sgu_ref, x_ref, wo_ref, g1_ref, b1_ref, x1_ref):
    tq = q_ref.shape[0]
    d = x_ref.shape[1]
    nchunk = tq // BLOCK
    sub = ATTN_WO_ROWS // BLOCK
    nblk = nchunk // sub
    npiece = d // ATTN_WO_COLS
    i = pl.program_id(1)
    last = pl.num_programs(1) - 1
    col = lax.broadcasted_iota(jnp.int32, (1, 3 * BLOCK), 1)
    lo_pen = jnp.where((col < BLOCK) & (i == 0), NEG_INF, 0.0).astype(F32)
    hi_pen = jnp.where((col >= 2 * BLOCK) & (i == last), NEG_INF, 0.0).astype(F32)
    edge_pen = lambda c: ([lo_pen] if c == 0 else []) + ([hi_pen] if c == nchunk - 1 else [])
    kk, vv = [], []
    for h in range(N_KV_HEADS):
        hc = slice(h * HEAD_DIM, (h + 1) * HEAD_DIM)
        kk.append(jnp.concatenate([kp_ref[:, hc], km_ref[:, hc], kn_ref[:, hc]], axis=0))
        vv.append(jnp.concatenate([vp_ref[:, hc], vm_ref[:, hc], vn_ref[:, hc]], axis=0))

    heads = {}

    def attend(chunks):
        units = [(c, h) for c in chunks for h in range(N_KV_HEADS)]
        heads.update(_attn_units(q_ref, kk, vv, pen_ref, sink_ref, units, edge_pen))

    def wo_piece(blk, attn, n):
        rows = slice(blk * ATTN_WO_ROWS, (blk + 1) * ATTN_WO_ROWS)
        cols = slice(n * ATTN_WO_COLS, (n + 1) * ATTN_WO_COLS)
        piece = jnp.dot(attn, wo_ref[:ATTN_WIDTH, cols], preferred_element_type=F32)
        return piece + jnp.dot(sgu_ref[rows, :], wo_ref[ATTN_WIDTH:, cols], preferred_element_type=F32)

    def norm_store(blk, pieces):
        rows = slice(blk * ATTN_WO_ROWS, (blk + 1) * ATTN_WO_ROWS)
        mix = jnp.concatenate(pieces, axis=1)
        x1_ref[rows, :] = _layer_norm(alpha * x_ref[rows, :] + mix, g1_ref[...], b1_ref[...])

    for c in range(sub):
        attend([c])
    prev_pieces = None
    for blk in range(nblk):
        attn = jnp.concatenate(
            [jnp.concatenate([heads[(c, hd)] for hd in range(N_HEADS)], axis=1)
             for c in range(blk * sub, (blk + 1) * sub)], axis=0)
        pending = list(range((blk + 1) * sub, (blk + 2) * sub)) if blk + 1 < nblk else []
        pieces = []
        for n in range(npiece):
            pieces.append(wo_piece(blk, attn, n))
            if n % 2 == 0 and n // 2 < len(pending):
                attend([pending[n // 2]])
            if n == 1 and prev_pieces is not None:
                norm_store(blk - 1, prev_pieces)
        prev_pieces = pieces
    norm_store(nblk - 1, prev_pieces)


def _alibi_penalty():
    slopes = 2.0 ** (-8.0 * np.arange(1, N_HEADS + 1, dtype=np.float64) / N_HEADS)
    rel = np.arange(3 * BLOCK)[None, :] - BLOCK - np.arange(BLOCK)[:, None]
    dist = np.abs(rel).astype(np.float64)
    pen = np.where(dist[None] <= WINDOW, -slopes[:, None, None] * dist[None] * LOG2E, NEG_INF)
    return pen.reshape(N_HEADS * BLOCK, 3 * BLOCK).astype(np.float32)


def _attn_wo_call(alpha, qkv, sink, sgu, x, w_o, g1, b1):
    b, s, d = x.shape
    tq = ATTN_TQ
    r = tq // BLOCK
    nc = s // BLOCK
    kcol = ATTN_WIDTH // KV_WIDTH
    vcol = kcol + 1
    prev = lambda i: jnp.maximum(i * r - 1, 0)
    nxt = lambda i: jnp.minimum((i + 1) * r, nc - 1)
    pen = jnp.asarray(_alibi_penalty())
    const = lambda *shape: pl.BlockSpec(shape, lambda bi, i: (0,) * len(shape), pipeline_mode=pl.Buffered(1))
    tile = lambda width: pl.BlockSpec((None, tq, width), lambda bi, i: (bi, i, 0))
    return pl.pallas_call(
        functools.partial(_attn_wo_kernel, alpha),
        grid=(b, s // tq),
        in_specs=[
            pl.BlockSpec(memory_space=pltpu.SMEM),
            tile(ATTN_WIDTH),
            pl.BlockSpec((None, BLOCK, KV_WIDTH), lambda bi, i: (bi, prev(i), kcol)),
            pl.BlockSpec((None, tq, KV_WIDTH), lambda bi, i: (bi, i, kcol)),
            pl.BlockSpec((None, BLOCK, KV_WIDTH), lambda bi, i: (bi, nxt(i), kcol)),
            pl.BlockSpec((None, BLOCK, KV_WIDTH), lambda bi, i: (bi, prev(i), vcol)),
            pl.BlockSpec((None, tq, KV_WIDTH), lambda bi, i: (bi, i, vcol)),
            pl.BlockSpec((None, BLOCK, KV_WIDTH), lambda bi, i: (bi, nxt(i), vcol)),
            const(*pen.shape),
            tile(SGU_WIDTH),
            tile(d),
            const(ATTN_WIDTH + SGU_WIDTH, d),
            const(1, d),
            const(1, d),
        ],
        out_specs=tile(d),
        out_shape=jax.ShapeDtypeStruct((b, s, d), F32),
        compiler_params=pltpu.CompilerParams(
            dimension_semantics=("arbitrary", "arbitrary"), vmem_limit_bytes=VMEM_LIMIT),
        name="attn_wo",
    )(sink, qkv, qkv, qkv, qkv, qkv, qkv, qkv, pen, sgu, x, w_o, g1, b1)


def _ffn_kernel(alpha, x1_ref, wg_ref, wu_ref, wd_ref, g2_ref, b2_ref, out_ref, x1b_ref):
    j = pl.program_id(1)
    last = pl.num_programs(1) - 1
    tm = out_ref.shape[0]

    def swiglu(xb):
        gate = jnp.dot(xb, wg_ref[...], preferred_element_type=F32)
        up = jnp.dot(xb, wu_ref[...], preferred_element_type=F32)
        return (gate * _sigmoid(gate) * up).astype(BF16)

    @pl.when(j == 0)
    def _():
        x1 = x1_ref[...]
        xb = x1.astype(BF16)
        x1b_ref[...] = xb
        out_ref[...] = alpha * x1 + jnp.dot(swiglu(xb), wd_ref[...], preferred_element_type=F32)

    @pl.when((j > 0) & (j < last))
    def _():
        out_ref[...] += jnp.dot(swiglu(x1b_ref[...]), wd_ref[...], preferred_element_type=F32)

    @pl.when(j == last)
    def _():
        act = swiglu(x1b_ref[...])
        for r in range(tm // FFN_LN_ROWS):
            rows = slice(r * FFN_LN_ROWS, (r + 1) * FFN_LN_ROWS)
            y = out_ref[rows, :] + jnp.dot(act[rows, :], wd_ref[...], preferred_element_type=F32)
            out_ref[rows, :] = _layer_norm(y, g2_ref[...], b2_ref[...])


def _ffn_call(alpha, x1, w_gate, w_up, w_down, g2, b2):
    t, d = x1.shape
    nchunk, _, tf = w_gate.shape
    tm = FFN_TM
    const = lambda *shape: pl.BlockSpec(shape, lambda i, j: (0,) * len(shape), pipeline_mode=pl.Buffered(1))
    return pl.pallas_call(
        functools.partial(_ffn_kernel, alpha),
        grid=(t // tm, nchunk),
        in_specs=[
            pl.BlockSpec((tm, d), lambda i, j: (i, 0)),
            pl.BlockSpec((None, d, tf), lambda i, j: (j, 0, 0)),
            pl.BlockSpec((None, d, tf), lambda i, j: (j, 0, 0)),
            pl.BlockSpec((tf, d), lambda i, j: (j, 0)),
            const(1, d),
            const(1, d),
        ],
        out_specs=pl.BlockSpec((tm, d), lambda i, j: (i, 0)),
        out_shape=jax.ShapeDtypeStruct((t, d), F32),
        scratch_shapes=[pltpu.VMEM((tm, d), BF16)],
        compiler_params=pltpu.CompilerParams(
            dimension_semantics=("arbitrary", "arbitrary"), vmem_limit_bytes=FFN_VMEM_LIMIT),
        name="ffn",
    )(x1, w_gate, w_up, w_down, g2, b2)


def _layer(alpha, x, w_in, ln_sgu_g, ln_sgu_b, w_s, bs_full, sink, g1, b1, g2, b2, late_weights):
    b, s, d = x.shape
    if late_weights[0].dtype == BF16:
        qkv, sgu = _proj_call(x.reshape(b * s, d), w_in, ln_sgu_g, ln_sgu_b, w_s, bs_full)
    else:
        qkv, sgu, *late_weights = _proj_call(x.reshape(b * s, d), w_in, ln_sgu_g, ln_sgu_b, w_s, bs_full,
                                             cast=late_weights, chunk_cols=(None, FFN_TF, FFN_TF, None))
    w_o, w_gate, w_up, w_down = late_weights
    x1 = _attn_wo_call(alpha, qkv.reshape(b, s, QKV_WIDTH), sink, sgu.reshape(b, s, SGU_WIDTH), x, w_o, g1, b1)
    out = _ffn_call(alpha, x1.reshape(b * s, d), w_gate, w_up, w_down, g2, b2)
    return out.reshape(b, s, d), tuple(late_weights)


def kernel(x_prompt, x_sample, w_in, ln_sgu_g, ln_sgu_b, w_s, b_s, attn_sink, w_o, ln1_g, ln1_b,
           w_gate, w_up, w_down, ln2_g, ln2_b):
    depth = w_in.shape[0]
    alpha = (2.0 * depth) ** 0.25
    row = lambda v: v.reshape(1, -1).astype(F32)
    y_prompt, y_sample = x_prompt, x_sample
    for l in range(depth):
        bs_full = jnp.repeat(b_s[l].T.astype(F32), SGU_GROUP_DIM, axis=1)
        params = (
            w_in[l].astype(BF16), row(ln_sgu_g[l]), row(ln_sgu_b[l]), w_s[l].astype(BF16), bs_full,
            attn_sink[l].astype(F32), row(ln1_g[l]), row(ln1_b[l]), row(ln2_g[l]), row(ln2_b[l]),
        )
        late_f32 = tuple(w[l].astype(F32) for w in (w_o, w_gate, w_up, w_down))
        y_prompt, late_bf16 = _layer(alpha, y_prompt, *params, late_f32)
        y_sample, _ = _layer(alpha, y_sample, *params, late_bf16)
    return (y_prompt, y_sample)
```

```python
import functools
import math

import numpy as np
import jax
import jax.numpy as jnp
from jax import lax
from jax.experimental import pallas as pl
from jax.experimental.pallas import tpu as pltpu

HEAD_DIM = 128
N_HEADS = 8
N_KV_HEADS = 2
GQA = N_HEADS // N_KV_HEADS
ATTN_WIDTH = N_HEADS * HEAD_DIM
KV_WIDTH = N_KV_HEADS * HEAD_DIM
QKV_WIDTH = ATTN_WIDTH + 2 * KV_WIDTH
N_SGU_GROUPS = 8
SGU_GROUP_DIM = 128
SGU_WIDTH = N_SGU_GROUPS * SGU_GROUP_DIM
BLOCK = 128
WINDOW = 128
LN_EPS = 1e-5
NEG_INF = -1e30
LOG2E = math.log2(math.e)
BF16_SUBLANES = 16

PROJ_TM = 512
ATTN_TQ = 512
ATTN_WO_ROWS = 256
ATTN_WO_COLS = 512
FFN_TM = 1024
FFN_TF = 512
FFN_LN_ROWS = 256
VMEM_LIMIT = 56 * 1024 * 1024
FFN_VMEM_LIMIT = 62 * 1024 * 1024

F32 = jnp.float32
BF16 = jnp.bfloat16


def _layer_norm(x, g, b):
    mu = jnp.mean(x, axis=-1, keepdims=True)
    xc = x - mu
    var = jnp.mean(xc * xc, axis=-1, keepdims=True)
    return xc * lax.rsqrt(var + LN_EPS) * g + b


def _gelu_tanh(x):
    c = math.sqrt(2.0 / math.pi)
    return x * (0.5 * (1.0 + jnp.tanh(c * (x + 0.044715 * (x * x * x)))))


def _sigmoid(x):
    return 0.5 * (jnp.tanh(0.5 * x) + 1.0)


def _proj_kernel(n_cast, x_ref, w_ref, lng_ref, lnb_ref, ws_ref, bs_ref, *refs):
    cast_in, (qkv_ref, sgu_ref), cast_out = refs[:n_cast], refs[n_cast:n_cast + 2], refs[n_cast + 2:]
    tm = x_ref.shape[0]
    xb = x_ref[...].astype(BF16)
    gv = _gelu_tanh(jnp.dot(xb, w_ref[:, QKV_WIDTH + SGU_WIDTH:], preferred_element_type=F32))
    gv = _layer_norm(gv, lng_ref[...], lnb_ref[...]).astype(BF16)
    u = _gelu_tanh(jnp.dot(xb, w_ref[:, QKV_WIDTH:QKV_WIDTH + SGU_WIDTH], preferred_element_type=F32))
    q = jnp.dot(xb, w_ref[:, :ATTN_WIDTH], preferred_element_type=F32)
    qkv_ref[:, :ATTN_WIDTH] = (q * (LOG2E / math.sqrt(HEAD_DIM))).astype(BF16)
    for c in range(tm // BLOCK):
        rows = slice(c * BLOCK, (c + 1) * BLOCK)
        for g in range(N_SGU_GROUPS):
            cols = slice(g * SGU_GROUP_DIM, (g + 1) * SGU_GROUP_DIM)
            mixed = jnp.dot(ws_ref[g], gv[rows, cols], preferred_element_type=F32) + bs_ref[:, cols]
            sgu_ref[rows, cols] = (u[rows, cols] * mixed).astype(BF16)
    kv = jnp.dot(xb, w_ref[:, ATTN_WIDTH:QKV_WIDTH], preferred_element_type=F32)
    qkv_ref[:, ATTN_WIDTH:] = kv.astype(BF16)
    for src, dst in zip(cast_in, cast_out):
        if len(dst.shape) == 2:
            dst[...] = src[...].astype(BF16)
        else:
            width = dst.shape[2]
            for c in range(dst.shape[0]):
                dst[c] = src[:, c * width:(c + 1) * width].astype(BF16)


def _proj_call(x2, w_in, ln_g, ln_b, w_s, bs_full, cast=(), chunk_cols=()):
    t, d = x2.shape
    tm = PROJ_TM
    steps = t // tm
    in_w = w_in.shape[1]
    const = lambda *shape: pl.BlockSpec(shape, lambda i: (0,) * len(shape), pipeline_mode=pl.Buffered(1))
    for w in cast:
        assert w.shape[0] % (steps * BF16_SUBLANES) == 0, (w.shape, steps)
    slab = lambda w: pl.BlockSpec((w.shape[0] // steps, w.shape[1]), lambda i: (i, 0))
    cast_specs, cast_shapes = [], []
    for w, cw in zip(cast, chunk_cols):
        if cw is None:
            cast_specs.append(slab(w))
            cast_shapes.append(jax.ShapeDtypeStruct(w.shape, BF16))
        else:
            cast_specs.append(pl.BlockSpec((w.shape[1] // cw, w.shape[0] // steps, cw), lambda i: (0, i, 0)))
            cast_shapes.append(jax.ShapeDtypeStruct((w.shape[1] // cw, w.shape[0], cw), BF16))
    return pl.pallas_call(
        functools.partial(_proj_kernel, len(cast)),
        grid=(steps,),
        in_specs=[
            pl.BlockSpec((tm, d), lambda i: (i, 0)),
            const(d, in_w),
            const(1, SGU_WIDTH),
            const(1, SGU_WIDTH),
            const(N_SGU_GROUPS, BLOCK, BLOCK),
            const(BLOCK, SGU_WIDTH),
        ] + [slab(w) for w in cast],
        out_specs=[
            pl.BlockSpec((tm, QKV_WIDTH), lambda i: (i, 0)),
            pl.BlockSpec((tm, SGU_WIDTH), lambda i: (i, 0)),
        ] + cast_specs,
        out_shape=[
            jax.ShapeDtypeStruct((t, QKV_WIDTH), BF16),
            jax.ShapeDtypeStruct((t, SGU_WIDTH), BF16),
        ] + cast_shapes,
        compiler_params=pltpu.CompilerParams(
            dimension_semantics=("arbitrary",), vmem_limit_bytes=VMEM_LIMIT),
        name="proj_sgu",
    )(x2, w_in, ln_g, ln_b, w_s, bs_full, *cast)


def _attn_units(q_ref, kk, vv, pen_ref, sink_ref, units, edge_pen):
    scores = []
    for c, h in units:
        rows = slice(c * BLOCK, (c + 1) * BLOCK)
        qs = jnp.concatenate(
            [q_ref[rows, (h * GQA + g) * HEAD_DIM:(h * GQA + g + 1) * HEAD_DIM] for g in range(GQA)], axis=0)
        kc = kk[h][c * BLOCK:(c + 3) * BLOCK]
        s = lax.dot_general(qs, kc, (((1,), (1,)), ((), ())), preferred_element_type=F32)
        s = s + pen_ref[h * GQA * BLOCK:(h + 1) * GQA * BLOCK, :]
        for pen in edge_pen(c):
            s = s + pen
        scores.append(s)
    probs = []
    for (c, h), s in zip(units, scores):
        ps, rdens = [], []
        for g in range(GQA):
            sg = s[g * BLOCK:(g + 1) * BLOCK]
            sk = sink_ref[h * GQA + g] * LOG2E
            m = jnp.maximum(jnp.max(sg, axis=-1, keepdims=True), sk)
            p = jnp.exp2(sg - m)
            rdens.append(1.0 / (jnp.sum(p, axis=-1, keepdims=True) + jnp.exp2(sk - m)))
            ps.append(p.astype(BF16))
        probs.append((jnp.concatenate(ps, axis=0), rdens))
    out = {}
    for (c, h), (p, rdens) in zip(units, probs):
        o = jnp.dot(p, vv[h][c * BLOCK:(c + 3) * BLOCK], preferred_element_type=F32)
        for g in range(GQA):
            out[(c, h * GQA + g)] = (o[g * BLOCK:(g + 1) * BLOCK] * rdens[g]).astype(BF16)
    return out


def _attn_wo_kernel(alpha, sink_ref, q_ref, kp_ref, km_ref, kn_ref, vp_ref, vm_ref, vn_ref, pen_ref,
                    sgu_ref, x_ref, wo_ref, y_ref):
    tq = q_ref.shape[0]
    d = x_ref.shape[1]
    nchunk = tq // BLOCK
    sub = ATTN_WO_ROWS // BLOCK
    nblk = nchunk // sub
    npiece = d // ATTN_WO_COLS
    i = pl.program_id(1)
    last = pl.num_programs(1) - 1
    col = lax.broadcasted_iota(jnp.int32, (1, 3 * BLOCK), 1)
    lo_pen = jnp.where((col < BLOCK) & (i == 0), NEG_INF, 0.0).astype(F32)
    hi_pen = jnp.where((col >= 2 * BLOCK) & (i == last), NEG_INF, 0.0).astype(F32)
    edge_pen = lambda c: ([lo_pen] if c == 0 else []) + ([hi_pen] if c == nchunk - 1 else [])
    kk, vv = [], []
    for h in range(N_KV_HEADS):
        hc = slice(h * HEAD_DIM, (h + 1) * HEAD_DIM)
        kk.append(jnp.concatenate([kp_ref[:, hc], km_ref[:, hc], kn_ref[:, hc]], axis=0))
        vv.append(jnp.concatenate([vp_ref[:, hc], vm_ref[:, hc], vn_ref[:, hc]], axis=0))

    heads = {}

    def attend(chunks):
        units = [(c, h) for c in chunks for h in range(N_KV_HEADS)]
        heads.update(_attn_units(q_ref, kk, vv, pen_ref, sink_ref, units, edge_pen))

    def wo_piece(blk, attn, n):
        rows = slice(blk * ATTN_WO_ROWS, (blk + 1) * ATTN_WO_ROWS)
        cols = slice(n * ATTN_WO_COLS, (n + 1) * ATTN_WO_COLS)
        mix = jnp.dot(attn, wo_ref[:ATTN_WIDTH, cols], preferred_element_type=F32)
        mix = mix + jnp.dot(sgu_ref[rows, :], wo_ref[ATTN_WIDTH:, cols], preferred_element_type=F32)
        y_ref[rows, cols] = alpha * x_ref[rows, cols] + mix

    for c in range(sub):
        attend([c])
    for blk in range(nblk):
        attn = jnp.concatenate(
            [jnp.concatenate([heads[(c, hd)] for hd in range(N_HEADS)], axis=1)
             for c in range(blk * sub, (blk + 1) * sub)], axis=0)
        pending = list(range((blk + 1) * sub, (blk + 2) * sub)) if blk + 1 < nblk else []
        for n in range(npiece):
            wo_piece(blk, attn, n)
            if n % 2 == 0 and n // 2 < len(pending):
                attend([pending[n // 2]])


def _alibi_penalty():
    slopes = 2.0 ** (-8.0 * np.arange(1, N_HEADS + 1, dtype=np.float64) / N_HEADS)
    rel = np.arange(3 * BLOCK)[None, :] - BLOCK - np.arange(BLOCK)[:, None]
    dist = np.abs(rel).astype(np.float64)
    pen = np.where(dist[None] <= WINDOW, -slopes[:, None, None] * dist[None] * LOG2E, NEG_INF)
    return pen.reshape(N_HEADS * BLOCK, 3 * BLOCK).astype(np.float32)


def _attn_wo_call(alpha, qkv, sink, sgu, x, w_o):
    b, s, d = x.shape
    tq = ATTN_TQ
    r = tq // BLOCK
    nc = s // BLOCK
    kcol = ATTN_WIDTH // KV_WIDTH
    vcol = kcol + 1
    prev = lambda i: jnp.maximum(i * r - 1, 0)
    nxt = lambda i: jnp.minimum((i + 1) * r, nc - 1)
    pen = jnp.asarray(_alibi_penalty())
    const = lambda *shape: pl.BlockSpec(shape, lambda bi, i: (0,) * len(shape), pipeline_mode=pl.Buffered(1))
    tile = lambda width: pl.BlockSpec((None, tq, width), lambda bi, i: (bi, i, 0))
    return pl.pallas_call(
        functools.partial(_attn_wo_kernel, alpha),
        grid=(b, s // tq),
        in_specs=[
            pl.BlockSpec(memory_space=pltpu.SMEM),
            tile(ATTN_WIDTH),
            pl.BlockSpec((None, BLOCK, KV_WIDTH), lambda bi, i: (bi, prev(i), kcol)),
            pl.BlockSpec((None, tq, KV_WIDTH), lambda bi, i: (bi, i, kcol)),
            pl.BlockSpec((None, BLOCK, KV_WIDTH), lambda bi, i: (bi, nxt(i), kcol)),
            pl.BlockSpec((None, BLOCK, KV_WIDTH), lambda bi, i: (bi, prev(i), vcol)),
            pl.BlockSpec((None, tq, KV_WIDTH), lambda bi, i: (bi, i, vcol)),
            pl.BlockSpec((None, BLOCK, KV_WIDTH), lambda bi, i: (bi, nxt(i), vcol)),
            const(*pen.shape),
            tile(SGU_WIDTH),
            tile(d),
            const(ATTN_WIDTH + SGU_WIDTH, d),
        ],
        out_specs=tile(d),
        out_shape=jax.ShapeDtypeStruct((b, s, d), F32),
        compiler_params=pltpu.CompilerParams(
            dimension_semantics=("arbitrary", "arbitrary"), vmem_limit_bytes=VMEM_LIMIT),
        name="attn_wo",
    )(sink, qkv, qkv, qkv, qkv, qkv, qkv, qkv, pen, sgu, x, w_o)


def _ffn_kernel(alpha, y_ref, g1_ref, b1_ref, wg_ref, wu_ref, wd_ref, g2_ref, b2_ref, out_ref, x1b_ref):
    j = pl.program_id(1)
    last = pl.num_programs(1) - 1
    tm = out_ref.shape[0]

    def swiglu(xb):
        gate = jnp.dot(xb, wg_ref[...], preferred_element_type=F32)
        up = jnp.dot(xb, wu_ref[...], preferred_element_type=F32)
        return (gate * _sigmoid(gate) * up).astype(BF16)

    @pl.when(j == 0)
    def _():
        for r in range(tm // FFN_LN_ROWS):
            rows = slice(r * FFN_LN_ROWS, (r + 1) * FFN_LN_ROWS)
            x1 = _layer_norm(y_ref[rows, :], g1_ref[...], b1_ref[...])
            xb = x1.astype(BF16)
            x1b_ref[rows, :] = xb
            out_ref[rows, :] = alpha * x1 + jnp.dot(swiglu(xb), wd_ref[...], preferred_element_type=F32)

    @pl.when((j > 0) & (j < last))
    def _():
        out_ref[...] += jnp.dot(swiglu(x1b_ref[...]), wd_ref[...], preferred_element_type=F32)

    @pl.when(j == last)
    def _():
        act = swiglu(x1b_ref[...])
        for r in range(tm // FFN_LN_ROWS):
            rows = slice(r * FFN_LN_ROWS, (r + 1) * FFN_LN_ROWS)
            y = out_ref[rows, :] + jnp.dot(act[rows, :], wd_ref[...], preferred_element_type=F32)
            out_ref[rows, :] = _layer_norm(y, g2_ref[...], b2_ref[...])


def _ffn_call(alpha, y, g1, b1, w_gate, w_up, w_down, g2, b2):
    t, d = y.shape
    nchunk, _, tf = w_gate.shape
    tm = FFN_TM
    const = lambda *shape: pl.BlockSpec(shape, lambda i, j: (0,) * len(shape), pipeline_mode=pl.Buffered(1))
    return pl.pallas_call(
        functools.partial(_ffn_kernel, alpha),
        grid=(t // tm, nchunk),
        in_specs=[
            pl.BlockSpec((tm, d), lambda i, j: (i, 0)),
            const(1, d),
            const(1, d),
            pl.BlockSpec((None, d, tf), lambda i, j: (j, 0, 0)),
            pl.BlockSpec((None, d, tf), lambda i, j: (j, 0, 0)),
            pl.BlockSpec((tf, d), lambda i, j: (j, 0)),
            const(1, d),
            const(1, d),
        ],
        out_specs=pl.BlockSpec((tm, d), lambda i, j: (i, 0)),
        out_shape=jax.ShapeDtypeStruct((t, d), F32),
        scratch_shapes=[pltpu.VMEM((tm, d), BF16)],
        compiler_params=pltpu.CompilerParams(
            dimension_semantics=("arbitrary", "arbitrary"), vmem_limit_bytes=FFN_VMEM_LIMIT),
        name="ffn",
    )(y, g1, b1, w_gate, w_up, w_down, g2, b2)


def _layer(alpha, x, w_in, ln_sgu_g, ln_sgu_b, w_s, bs_full, sink, g1, b1, g2, b2, late_weights):
    b, s, d = x.shape
    if late_weights[0].dtype == BF16:
        qkv, sgu = _proj_call(x.reshape(b * s, d), w_in, ln_sgu_g, ln_sgu_b, w_s, bs_full)
    else:
        qkv, sgu, *late_weights = _proj_call(x.reshape(b * s, d), w_in, ln_sgu_g, ln_sgu_b, w_s, bs_full,
                                             cast=late_weights, chunk_cols=(None, FFN_TF, FFN_TF, None))
    w_o, w_gate, w_up, w_down = late_weights
    y = _attn_wo_call(alpha, qkv.reshape(b, s, QKV_WIDTH), sink, sgu.reshape(b, s, SGU_WIDTH), x, w_o)
    out = _ffn_call(alpha, y.reshape(b * s, d), g1, b1, w_gate, w_up, w_down, g2, b2)
    return out.reshape(b, s, d), tuple(late_weights)


def kernel(x_prompt, x_sample, w_in, ln_sgu_g, ln_sgu_b, w_s, b_s, attn_sink, w_o, ln1_g, ln1_b,
           w_gate, w_up, w_down, ln2_g, ln2_b):
    depth = w_in.shape[0]
    alpha = (2.0 * depth) ** 0.25
    row = lambda v: v.reshape(1, -1).astype(F32)
    y_prompt, y_sample = x_prompt, x_sample
    for l in range(depth):
        bs_full = jnp.repeat(b_s[l].T.astype(F32), SGU_GROUP_DIM, axis=1)
        params = (
            w_in[l].astype(BF16), row(ln_sgu_g[l]), row(ln_sgu_b[l]), w_s[l].astype(BF16), bs_full,
            attn_sink[l].astype(F32), row(ln1_g[l]), row(ln1_b[l]), row(ln2_g[l]), row(ln2_b[l]),
        )
        late_f32 = tuple(w[l].astype(F32) for w in (w_o, w_gate, w_up, w_down))
        y_prompt, late_bf16 = _layer(alpha, y_prompt, *params, late_f32)
        y_sample, _ = _layer(alpha, y_sample, *params, late_bf16)
    return (y_prompt, y_sample)
```

```python
import functools
import math

import numpy as np
import jax
import jax.numpy as jnp
from jax import lax
from jax.experimental import pallas as pl
from jax.experimental.pallas import tpu as pltpu

HEAD_DIM = 128
N_HEADS = 8
N_KV_HEADS = 2
GQA = N_HEADS // N_KV_HEADS
ATTN_WIDTH = N_HEADS * HEAD_DIM
KV_WIDTH = N_KV_HEADS * HEAD_DIM
QKV_WIDTH = ATTN_WIDTH + 2 * KV_WIDTH
N_SGU_GROUPS = 8
SGU_GROUP_DIM = 128
SGU_WIDTH = N_SGU_GROUPS * SGU_GROUP_DIM
BLOCK = 128
WINDOW = 128
LN_EPS = 1e-5
NEG_INF = -1e30
LOG2E = math.log2(math.e)
BF16_SUBLANES = 16

PROJ_TM = 512
ATTN_TQ = 512
ATTN_WO_ROWS = 256
ATTN_WO_COLS = 512
FFN_TM = 1024
FFN_TF = 512
FFN_LN_ROWS = 256
FFN_LN1_ROWS = 512
VMEM_LIMIT = 56 * 1024 * 1024
FFN_VMEM_LIMIT = 62 * 1024 * 1024

F32 = jnp.float32
BF16 = jnp.bfloat16


def _layer_norm(x, g, b):
    mu = jnp.mean(x, axis=-1, keepdims=True)
    xc = x - mu
    var = jnp.mean(xc * xc, axis=-1, keepdims=True)
    return xc * lax.rsqrt(var + LN_EPS) * g + b


def _gelu_tanh(x):
    c = math.sqrt(2.0 / math.pi)
    return x * (0.5 * (1.0 + jnp.tanh(c * (x + 0.044715 * (x * x * x)))))


def _sigmoid(x):
    return 0.5 * (jnp.tanh(0.5 * x) + 1.0)


def _proj_kernel(n_cast, x_ref, w_ref, lng_ref, lnb_ref, ws_ref, bs_ref, *refs):
    cast_in, (qkv_ref, sgu_ref), cast_out = refs[:n_cast], refs[n_cast:n_cast + 2], refs[n_cast + 2:]
    tm = x_ref.shape[0]
    xb = x_ref[...].astype(BF16)
    gv = _gelu_tanh(jnp.dot(xb, w_ref[:, QKV_WIDTH + SGU_WIDTH:], preferred_element_type=F32))
    gv = _layer_norm(gv, lng_ref[...], lnb_ref[...]).astype(BF16)
    u = _gelu_tanh(jnp.dot(xb, w_ref[:, QKV_WIDTH:QKV_WIDTH + SGU_WIDTH], preferred_element_type=F32))
    q = jnp.dot(xb, w_ref[:, :ATTN_WIDTH], preferred_element_type=F32)
    qkv_ref[:, :ATTN_WIDTH] = (q * (LOG2E / math.sqrt(HEAD_DIM))).astype(BF16)
    for c in range(tm // BLOCK):
        rows = slice(c * BLOCK, (c + 1) * BLOCK)
        for g in range(N_SGU_GROUPS):
            cols = slice(g * SGU_GROUP_DIM, (g + 1) * SGU_GROUP_DIM)
            mixed = jnp.dot(ws_ref[g], gv[rows, cols], preferred_element_type=F32) + bs_ref[:, cols]
            sgu_ref[rows, cols] = (u[rows, cols] * mixed).astype(BF16)
    kv = jnp.dot(xb, w_ref[:, ATTN_WIDTH:QKV_WIDTH], preferred_element_type=F32)
    qkv_ref[:, ATTN_WIDTH:] = kv.astype(BF16)
    for src, dst in zip(cast_in, cast_out):
        if len(dst.shape) == 2:
            dst[...] = src[...].astype(BF16)
        else:
            width = dst.shape[2]
            for c in range(dst.shape[0]):
                dst[c] = src[:, c * width:(c + 1) * width].astype(BF16)


def _proj_call(x2, w_in, ln_g, ln_b, w_s, bs_full, cast=(), chunk_cols=()):
    t, d = x2.shape
    tm = PROJ_TM
    steps = t // tm
    in_w = w_in.shape[1]
    const = lambda *shape: pl.BlockSpec(shape, lambda i: (0,) * len(shape), pipeline_mode=pl.Buffered(1))
    for w in cast:
        assert w.shape[0] % (steps * BF16_SUBLANES) == 0, (w.shape, steps)
    slab = lambda w: pl.BlockSpec((w.shape[0] // steps, w.shape[1]), lambda i: (i, 0))
    cast_specs, cast_shapes = [], []
    for w, cw in zip(cast, chunk_cols):
        if cw is None:
            cast_specs.append(slab(w))
            cast_shapes.append(jax.ShapeDtypeStruct(w.shape, BF16))
        else:
            cast_specs.append(pl.BlockSpec((w.shape[1] // cw, w.shape[0] // steps, cw), lambda i: (0, i, 0)))
            cast_shapes.append(jax.ShapeDtypeStruct((w.shape[1] // cw, w.shape[0], cw), BF16))
    return pl.pallas_call(
        functools.partial(_proj_kernel, len(cast)),
        grid=(steps,),
        in_specs=[
            pl.BlockSpec((tm, d), lambda i: (i, 0)),
            const(d, in_w),
            const(1, SGU_WIDTH),
            const(1, SGU_WIDTH),
            const(N_SGU_GROUPS, BLOCK, BLOCK),
            const(BLOCK, SGU_WIDTH),
        ] + [slab(w) for w in cast],
        out_specs=[
            pl.BlockSpec((tm, QKV_WIDTH), lambda i: (i, 0)),
            pl.BlockSpec((tm, SGU_WIDTH), lambda i: (i, 0)),
        ] + cast_specs,
        out_shape=[
            jax.ShapeDtypeStruct((t, QKV_WIDTH), BF16),
            jax.ShapeDtypeStruct((t, SGU_WIDTH), BF16),
        ] + cast_shapes,
        compiler_params=pltpu.CompilerParams(
            dimension_semantics=("arbitrary",), vmem_limit_bytes=VMEM_LIMIT),
        name="proj_sgu",
    )(x2, w_in, ln_g, ln_b, w_s, bs_full, *cast)


def _attn_units(q_ref, kk, vv, pen_ref, sink_ref, units, edge_pen):
    scores = []
    for c, h in units:
        rows = slice(c * BLOCK, (c + 1) * BLOCK)
        qs = jnp.concatenate(
            [q_ref[rows, (h * GQA + g) * HEAD_DIM:(h * GQA + g + 1) * HEAD_DIM] for g in range(GQA)], axis=0)
        kc = kk[h][c * BLOCK:(c + 3) * BLOCK]
        s = lax.dot_general(qs, kc, (((1,), (1,)), ((), ())), preferred_element_type=F32)
        s = s + pen_ref[h * GQA * BLOCK:(h + 1) * GQA * BLOCK, :]
        for pen in edge_pen(c):
            s = s + pen
        scores.append(s)
    probs = []
    for (c, h), s in zip(units, scores):
        ps, rdens = [], []
        for g in range(GQA):
            sg = s[g * BLOCK:(g + 1) * BLOCK]
            sk = sink_ref[h * GQA + g] * LOG2E
            m = jnp.maximum(jnp.max(sg, axis=-1, keepdims=True), sk)
            p = jnp.exp2(sg - m)
            rdens.append(1.0 / (jnp.sum(p, axis=-1, keepdims=True) + jnp.exp2(sk - m)))
            ps.append(p.astype(BF16))
        probs.append((jnp.concatenate(ps, axis=0), rdens))
    out = {}
    for (c, h), (p, rdens) in zip(units, probs):
        o = jnp.dot(p, vv[h][c * BLOCK:(c + 3) * BLOCK], preferred_element_type=F32)
        for g in range(GQA):
            out[(c, h * GQA + g)] = (o[g * BLOCK:(g + 1) * BLOCK] * rdens[g]).astype(BF16)
    return out


def _attn_wo_kernel(alpha, sink_ref, q_ref, kp_ref, km_ref, kn_ref, vp_ref, vm_ref, vn_ref, pen_ref,
                    sgu_ref, x_ref, wo_ref, y_ref):
    tq = q_ref.shape[0]
    d = x_ref.shape[1]
    nchunk = tq // BLOCK
    sub = ATTN_WO_ROWS // BLOCK
    nblk = nchunk // sub
    npiece = d // ATTN_WO_COLS
    i = pl.program_id(1)
    last = pl.num_programs(1) - 1
    col = lax.broadcasted_iota(jnp.int32, (1, 3 * BLOCK), 1)
    lo_pen = jnp.where((col < BLOCK) & (i == 0), NEG_INF, 0.0).astype(F32)
    hi_pen = jnp.where((col >= 2 * BLOCK) & (i == last), NEG_INF, 0.0).astype(F32)
    edge_pen = lambda c: ([lo_pen] if c == 0 else []) + ([hi_pen] if c == nchunk - 1 else [])
    kk, vv = [], []
    for h in range(N_KV_HEADS):
        hc = slice(h * HEAD_DIM, (h + 1) * HEAD_DIM)
        kk.append(jnp.concatenate([kp_ref[:, hc], km_ref[:, hc], kn_ref[:, hc]], axis=0))
        vv.append(jnp.concatenate([vp_ref[:, hc], vm_ref[:, hc], vn_ref[:, hc]], axis=0))

    heads = {}

    def attend(chunks):
        units = [(c, h) for c in chunks for h in range(N_KV_HEADS)]
        heads.update(_attn_units(q_ref, kk, vv, pen_ref, sink_ref, units, edge_pen))

    def piece(blk, n):
        return (slice(blk * ATTN_WO_ROWS, (blk + 1) * ATTN_WO_ROWS),
                slice(n * ATTN_WO_COLS, (n + 1) * ATTN_WO_COLS))

    base = {}

    def sgu_piece(blk, n):
        rows, cols = piece(blk, n)
        base[(blk, n)] = alpha * x_ref[rows, cols] + jnp.dot(
            sgu_ref[rows, :], wo_ref[ATTN_WIDTH:, cols], preferred_element_type=F32)

    def attn_piece(blk, attn, n):
        rows, cols = piece(blk, n)
        y_ref[rows, cols] = base[(blk, n)] + jnp.dot(attn, wo_ref[:ATTN_WIDTH, cols], preferred_element_type=F32)

    for n in range(npiece):
        sgu_piece(0, n)
        if n % 2 == 1:
            attend([n // 2])
    for c in range(npiece // 2, sub):
        attend([c])
    for blk in range(nblk):
        attn = jnp.concatenate(
            [jnp.concatenate([heads[(c, hd)] for hd in range(N_HEADS)], axis=1)
             for c in range(blk * sub, (blk + 1) * sub)], axis=0)
        pending = list(range((blk + 1) * sub, (blk + 2) * sub)) if blk + 1 < nblk else []
        for n in range(npiece):
            attn_piece(blk, attn, n)
            if blk + 1 < nblk:
                sgu_piece(blk + 1, n)
            if n % 2 == 0 and n // 2 < len(pending):
                attend([pending[n // 2]])


def _alibi_penalty():
    slopes = 2.0 ** (-8.0 * np.arange(1, N_HEADS + 1, dtype=np.float64) / N_HEADS)
    rel = np.arange(3 * BLOCK)[None, :] - BLOCK - np.arange(BLOCK)[:, None]
    dist = np.abs(rel).astype(np.float64)
    pen = np.where(dist[None] <= WINDOW, -slopes[:, None, None] * dist[None] * LOG2E, NEG_INF)
    return pen.reshape(N_HEADS * BLOCK, 3 * BLOCK).astype(np.float32)


def _attn_wo_call(alpha, qkv, sink, sgu, x, w_o):
    b, s, d = x.shape
    tq = ATTN_TQ
    r = tq // BLOCK
    nc = s // BLOCK
    kcol = ATTN_WIDTH // KV_WIDTH
    vcol = kcol + 1
    prev = lambda i: jnp.maximum(i * r - 1, 0)
    nxt = lambda i: jnp.minimum((i + 1) * r, nc - 1)
    pen = jnp.asarray(_alibi_penalty())
    const = lambda *shape: pl.BlockSpec(shape, lambda bi, i: (0,) * len(shape), pipeline_mode=pl.Buffered(1))
    tile = lambda width: pl.BlockSpec((None, tq, width), lambda bi, i: (bi, i, 0))
    return pl.pallas_call(
        functools.partial(_attn_wo_kernel, alpha),
        grid=(b, s // tq),
        in_specs=[
            pl.BlockSpec(memory_space=pltpu.SMEM),
            tile(ATTN_WIDTH),
            pl.BlockSpec((None, BLOCK, KV_WIDTH), lambda bi, i: (bi, prev(i), kcol)),
            pl.BlockSpec((None, tq, KV_WIDTH), lambda bi, i: (bi, i, kcol)),
            pl.BlockSpec((None, BLOCK, KV_WIDTH), lambda bi, i: (bi, nxt(i), kcol)),
            pl.BlockSpec((None, BLOCK, KV_WIDTH), lambda bi, i: (bi, prev(i), vcol)),
            pl.BlockSpec((None, tq, KV_WIDTH), lambda bi, i: (bi, i, vcol)),
            pl.BlockSpec((None, BLOCK, KV_WIDTH), lambda bi, i: (bi, nxt(i), vcol)),
            const(*pen.shape),
            tile(SGU_WIDTH),
            tile(d),
            const(ATTN_WIDTH + SGU_WIDTH, d),
        ],
        out_specs=tile(d),
        out_shape=jax.ShapeDtypeStruct((b, s, d), F32),
        compiler_params=pltpu.CompilerParams(
            dimension_semantics=("arbitrary", "arbitrary"), vmem_limit_bytes=VMEM_LIMIT),
        name="attn_wo",
    )(sink, qkv, qkv, qkv, qkv, qkv, qkv, qkv, pen, sgu, x, w_o)


def _ffn_kernel(alpha, y_ref, g1_ref, b1_ref, wg_ref, wu_ref, wd_ref, g2_ref, b2_ref, out_ref, x1b_ref):
    j = pl.program_id(1)
    last = pl.num_programs(1) - 1
    tm = out_ref.shape[0]

    def swiglu(xb):
        gate = jnp.dot(xb, wg_ref[...], preferred_element_type=F32)
        up = jnp.dot(xb, wu_ref[...], preferred_element_type=F32)
        return (gate * _sigmoid(gate) * up).astype(BF16)

    @pl.when(j == 0)
    def _():
        for r in range(tm // FFN_LN1_ROWS):
            rows = slice(r * FFN_LN1_ROWS, (r + 1) * FFN_LN1_ROWS)
            x1 = _layer_norm(y_ref[rows, :], g1_ref[...], b1_ref[...])
            xb = x1.astype(BF16)
            x1b_ref[rows, :] = xb
            out_ref[rows, :] = alpha * x1 + jnp.dot(swiglu(xb), wd_ref[...], preferred_element_type=F32)

    @pl.when((j > 0) & (j < last))
    def _():
        out_ref[...] += jnp.dot(swiglu(x1b_ref[...]), wd_ref[...], preferred_element_type=F32)

    @pl.when(j == last)
    def _():
        act = swiglu(x1b_ref[...])
        for r in range(tm // FFN_LN_ROWS):
            rows = slice(r * FFN_LN_ROWS, (r + 1) * FFN_LN_ROWS)
            y = out_ref[rows, :] + jnp.dot(act[rows, :], wd_ref[...], preferred_element_type=F32)
            out_ref[rows, :] = _layer_norm(y, g2_ref[...], b2_ref[...])


def _ffn_call(alpha, y, g1, b1, w_gate, w_up, w_down, g2, b2):
    t, d = y.shape
    nchunk, _, tf = w_gate.shape
    tm = FFN_TM
    const = lambda *shape: pl.BlockSpec(shape, lambda i, j: (0,) * len(shape), pipeline_mode=pl.Buffered(1))
    return pl.pallas_call(
        functools.partial(_ffn_kernel, alpha),
        grid=(t // tm, nchunk),
        in_specs=[
            pl.BlockSpec((tm, d), lambda i, j: (i, 0)),
            const(1, d),
            const(1, d),
            pl.BlockSpec((None, d, tf), lambda i, j: (j, 0, 0)),
            pl.BlockSpec((None, d, tf), lambda i, j: (j, 0, 0)),
            pl.BlockSpec((tf, d), lambda i, j: (j, 0)),
            const(1, d),
            const(1, d),
        ],
        out_specs=pl.BlockSpec((tm, d), lambda i, j: (i, 0)),
        out_shape=jax.ShapeDtypeStruct((t, d), F32),
        scratch_shapes=[pltpu.VMEM((tm, d), BF16)],
        compiler_params=pltpu.CompilerParams(
            dimension_semantics=("arbitrary", "arbitrary"), vmem_limit_bytes=FFN_VMEM_LIMIT),
        name="ffn",
    )(y, g1, b1, w_gate, w_up, w_down, g2, b2)


def _layer(alpha, x, w_in, ln_sgu_g, ln_sgu_b, w_s, bs_full, sink, g1, b1, g2, b2, late_weights):
    b, s, d = x.shape
    if late_weights[0].dtype == BF16:
        qkv, sgu = _proj_call(x.reshape(b * s, d), w_in, ln_sgu_g, ln_sgu_b, w_s, bs_full)
    else:
        qkv, sgu, *late_weights = _proj_call(x.reshape(b * s, d), w_in, ln_sgu_g, ln_sgu_b, w_s, bs_full,
                                             cast=late_weights, chunk_cols=(None, FFN_TF, FFN_TF, None))
    w_o, w_gate, w_up, w_down = late_weights
    y = _attn_wo_call(alpha, qkv.reshape(b, s, QKV_WIDTH), sink, sgu.reshape(b, s, SGU_WIDTH), x, w_o)
    out = _ffn_call(alpha, y.reshape(b * s, d), g1, b1, w_gate, w_up, w_down, g2, b2)
    return out.reshape(b, s, d), tuple(late_weights)


def kernel(x_prompt, x_sample, w_in, ln_sgu_g, ln_sgu_b, w_s, b_s, attn_sink, w_o, ln1_g, ln1_b,
           w_gate, w_up, w_down, ln2_g, ln2_b):
    depth = w_in.shape[0]
    alpha = (2.0 * depth) ** 0.25
    row = lambda v: v.reshape(1, -1).astype(F32)
    y_prompt, y_sample = x_prompt, x_sample
    for l in range(depth):
        bs_full = jnp.repeat(b_s[l].T.astype(F32), SGU_GROUP_DIM, axis=1)
        params = (
            w_in[l].astype(BF16), row(ln_sgu_g[l]), row(ln_sgu_b[l]), w_s[l].astype(BF16), bs_full,
            attn_sink[l].astype(F32), row(ln1_g[l]), row(ln1_b[l]), row(ln2_g[l]), row(ln2_b[l]),
        )
        late_f32 = tuple(w[l].astype(F32) for w in (w_o, w_gate, w_up, w_down))
        y_prompt, late_bf16 = _layer(alpha, y_prompt, *params, late_f32)
        y_sample, _ = _layer(alpha, y_sample, *params, late_bf16)
    return (y_prompt, y_sample)
```
